```python
import math
import jax, jax.numpy as jnp
from jax import lax
import numpy as np

D_MODEL = 1024
BATCH = 8
SEQ = 2048
DEPTH = 4
DEC_BATCH = 128
DEC_SEQ = 1
PAST_LEN = 16384
PAGE_SIZE = 128

GDN_HEADS = D_MODEL // 256
GDN_DK = 128
GDN_DV = 128
GDN_QK_W = GDN_HEADS * GDN_DK
GDN_V_W = GDN_HEADS * GDN_DV
GDN_CONV = 4
GDN_CONV_CH = 2 * GDN_QK_W + GDN_V_W
SSM_W = D_MODEL
SSM_HEADDIM = 64
SSM_HEADS = SSM_W // SSM_HEADDIM
SSM_GROUPS = 2
SSM_RPG = SSM_HEADS // SSM_GROUPS
SSM_STATE = 128
SSM_CONV = 4
SSM_XBC = SSM_W + 2 * SSM_GROUPS * SSM_STATE
CFM_W = D_MODEL // 2
CFM_KERNEL = 31
N_BRANCH = 3
CHUNK = 64
NORM_EPS = 1e-6
IN_SPLITS = (GDN_QK_W, GDN_QK_W, GDN_V_W, GDN_V_W, GDN_HEADS, GDN_HEADS,
             SSM_W, SSM_XBC, SSM_HEADS,
             2 * CFM_W, CFM_W,
             N_BRANCH * D_MODEL)
IN_DIM = sum(IN_SPLITS)

kernel_name = 'hybrid_gdn_ssd_conformer_gated_merge_step'


def _rmsnorm(x, w):
    xf = x.astype(jnp.float32)
    y = xf * lax.rsqrt(jnp.mean(xf * xf, axis=-1, keepdims=True) + NORM_EPS)
    return (y * w.astype(jnp.float32)).astype(x.dtype)


def _layernorm(x, w, b):
    xf = x.astype(jnp.float32)
    mu = jnp.mean(xf, axis=-1, keepdims=True)
    xc = xf - mu
    var = jnp.mean(xc * xc, axis=-1, keepdims=True)
    y = xc * lax.rsqrt(var + NORM_EPS) * w.astype(jnp.float32) + b.astype(jnp.float32)
    return y.astype(x.dtype)


def _l2norm(x):
    return x * lax.rsqrt(jnp.sum(x * x, axis=-1, keepdims=True) + NORM_EPS)


def _causal_dwconv(x, buf, w, b=None):
    k = w.shape[0]
    xp = jnp.concatenate([buf.astype(x.dtype), x], axis=1)
    y = lax.conv_general_dilated(xp, w[:, None, :].astype(x.dtype), (1,), 'VALID',
                                 dimension_numbers=('NWC', 'WIO', 'NWC'),
                                 feature_group_count=x.shape[-1])
    if b is not None:
        y = y + b.astype(x.dtype)
    return y, xp[:, xp.shape[1] - (k - 1):]


def _gdn_chunked(q, k, v, g, beta, s0):
    bsz, l, h, _ = q.shape
    dv = v.shape[-1]
    nc = l // CHUNK

    def blk(t):
        t = t.reshape((bsz, nc, CHUNK, h) + t.shape[3:])
        return jnp.moveaxis(t, (1, 3), (0, 2))

    q, k, v, g, beta = blk(q), blk(k), blk(v), blk(g), blk(beta)
    gam = jnp.cumsum(g, axis=-1)
    idx = jnp.arange(CHUNK)
    incl = idx[:, None] >= idx[None, :]
    strict = idx[:, None] > idx[None, :]
    diff = gam[..., :, None] - gam[..., None, :]
    dec_incl = jnp.exp(jnp.where(incl, diff, -jnp.inf))
    dec_strict = jnp.where(strict, dec_incl, 0.0)
    kb = k * beta[..., None]
    lmat = jnp.einsum('cbhtd,cbhsd->cbhts', kb, k) * dec_strict + jnp.eye(CHUNK, dtype=q.dtype)
    eg = jnp.exp(gam)[..., None]
    w_blk = lax.linalg.triangular_solve(lmat, kb * eg, left_side=True, lower=True, unit_diagonal=True)
    u_blk = lax.linalg.triangular_solve(lmat, v * beta[..., None], left_side=True, lower=True, unit_diagonal=True)
    attn = jnp.einsum('cbhtd,cbhsd->cbhts', q, k) * dec_incl
    q_g = q * eg
    k_d = k * jnp.exp(gam[..., -1:] - gam)[..., None]
    dec_tot = jnp.exp(gam[..., -1])

    def step(s, inp):
        w_c, u_c, a_c, qg_c, kd_c, d_c = inp
        u = u_c - jnp.einsum('bhtd,bhde->bhte', w_c, s)
        o = jnp.einsum('bhtd,bhde->bhte', qg_c, s) + jnp.einsum('bhts,bhse->bhte', a_c, u)
        s = s * d_c[..., None, None] + jnp.einsum('bhtd,bhte->bhde', kd_c, u)
        return s, o

    s_fin, o = lax.scan(step, s0, (w_blk, u_blk, attn, q_g, k_d, dec_tot))
    o = jnp.moveaxis(o, (0, 2), (1, 3)).reshape(bsz, l, h, dv)
    return o, s_fin


def _gdn_recurrent(q, k, v, g, beta, s0):
    def step(s, inp):
        q_t, k_t, v_t, g_t, b_t = inp
        s = s * jnp.exp(g_t)[..., None, None]
        v_old = jnp.einsum('bhd,bhde->bhe', k_t, s)
        s = s + jnp.einsum('bhd,bhe->bhde', k_t, b_t[..., None] * (v_t - v_old))
        return s, jnp.einsum('bhd,bhde->bhe', q_t, s)

    seq = (jnp.moveaxis(q, 1, 0), jnp.moveaxis(k, 1, 0), jnp.moveaxis(v, 1, 0),
           jnp.moveaxis(g, 1, 0), jnp.moveaxis(beta, 1, 0))
    s_fin, o = lax.scan(step, s0, seq)
    return jnp.moveaxis(o, 0, 1), s_fin


def _ssd_chunked(x, dt, a, bm, cm, s0):
    bsz, l = x.shape[:2]
    nc = l // CHUNK

    def blk(t):
        return jnp.moveaxis(t.reshape((bsz, nc, CHUNK) + t.shape[2:]), 1, 0)

    x, dt, bm, cm = blk(x), blk(dt), blk(bm), blk(cm)
    gam = jnp.cumsum(dt * a, axis=2)
    idx = jnp.arange(CHUNK)
    incl = (idx[:, None] >= idx[None, :])[:, :, None, None]
    diff = gam[:, :, :, None] - gam[:, :, None, :]
    dec = jnp.exp(jnp.where(incl, diff, -jnp.inf))
    cb = jnp.einsum('cbtgn,cbsgn->cbtsg', cm, bm)
    xdt = x * dt[..., None]
    y_intra = jnp.einsum('cbtsgr,cbsgrp->cbtgrp', cb[..., None] * dec, xdt)
    dec_end = jnp.exp(gam[:, :, -1:] - gam)
    chunk_state = jnp.einsum('cbsgr,cbsgn,cbsgrp->cbgrpn', dec_end, bm, xdt)
    dec_tot = jnp.exp(gam[:, :, -1])

    def step(s, inp):
        cs, d = inp
        return s * d[..., None, None] + cs, s

    s_fin, s_prev = lax.scan(step, s0, (chunk_state, dec_tot))
    y_inter = jnp.einsum('cbtgn,cbgrpn,cbtgr->cbtgrp', cm, s_prev, jnp.exp(gam))
    y = jnp.moveaxis(y_intra + y_inter, 0, 1).reshape((bsz, l) + x.shape[3:])
    return y, s_fin


def _ssd_recurrent(x, dt, a, bm, cm, s0):
    def step(s, inp):
        x_t, dt_t, b_t, c_t = inp
        s = s * jnp.exp(dt_t * a)[..., None, None] + jnp.einsum('bgr,bgrp,bgn->bgrpn', dt_t, x_t, b_t)
        return s, jnp.einsum('bgrpn,bgn->bgrp', s, c_t)

    seq = (jnp.moveaxis(x, 1, 0), jnp.moveaxis(dt, 1, 0), jnp.moveaxis(bm, 1, 0), jnp.moveaxis(cm, 1, 0))
    s_fin, y = lax.scan(step, s0, seq)
    return jnp.moveaxis(y, 0, 1), s_fin


def _layer(x, p, st, chunked):
    f32 = jnp.float32
    bsz, l, _ = x.shape
    gdn_s, gdn_buf, ssm_s, ssm_buf, cfm_buf = st
    h = _rmsnorm(x, p['norm_w'])
    proj = h @ p['w_in']
    cuts = np.cumsum(IN_SPLITS)[:-1].tolist()
    q, k, v, z_a, b_a, a_a, z_b, xbc, dt_b, glu_in, gate_c, gate_in = jnp.split(proj, cuts, axis=-1)

    qkv, gdn_buf = _causal_dwconv(jnp.concatenate([q, k, v], axis=-1), gdn_buf, p['gdn_conv_w'])
    qkv = jax.nn.silu(qkv.astype(f32))
    q, k, v = jnp.split(qkv, [GDN_QK_W, 2 * GDN_QK_W], axis=-1)
    q = _l2norm(q.reshape(bsz, l, GDN_HEADS, GDN_DK)) * (GDN_DK ** -0.5)
    k = _l2norm(k.reshape(bsz, l, GDN_HEADS, GDN_DK))
    v = v.reshape(bsz, l, GDN_HEADS, GDN_DV)
    beta = jax.nn.sigmoid(b_a.astype(f32))
    g = -jnp.exp(p['gdn_a_log'].astype(f32)) * jax.nn.softplus(a_a.astype(f32) + p['gdn_dt_bias'].astype(f32))
    gdn_fn = _gdn_chunked if chunked else _gdn_recurrent
    o, gdn_s = gdn_fn(q, k, v, g, beta, gdn_s.astype(f32))
    o = _rmsnorm(o, p['gdn_norm_w']) * jax.nn.silu(z_a.astype(f32)).reshape(bsz, l, GDN_HEADS, GDN_DV)
    br_a = o.reshape(bsz, l, GDN_V_W).astype(x.dtype) @ p['gdn_w_o']

    xbc, ssm_buf = _causal_dwconv(xbc, ssm_buf, p['ssm_conv_w'], p['ssm_conv_b'])
    xbc = jax.nn.silu(xbc.astype(f32))
    xs, bm, cm = jnp.split(xbc, [SSM_W, SSM_W + SSM_GROUPS * SSM_STATE], axis=-1)
    xs = xs.reshape(bsz, l, SSM_GROUPS, SSM_RPG, SSM_HEADDIM)
    bm = bm.reshape(bsz, l, SSM_GROUPS, SSM_STATE)
    cm = cm.reshape(bsz, l, SSM_GROUPS, SSM_STATE)
    dt = jax.nn.softplus(dt_b.astype(f32) + p['ssm_dt_bias'].astype(f32)).reshape(bsz, l, SSM_GROUPS, SSM_RPG)
    a = -jnp.exp(p['ssm_a_log'].astype(f32)).reshape(SSM_GROUPS, SSM_RPG)
    s0 = ssm_s.astype(f32).reshape(bsz, SSM_GROUPS, SSM_RPG, SSM_HEADDIM, SSM_STATE)
    ssd_fn = _ssd_chunked if chunked else _ssd_recurrent
    y, s_fin = ssd_fn(xs, dt, a, bm, cm, s0)
    y = y + p['ssm_d'].astype(f32).reshape(SSM_GROUPS, SSM_RPG)[:, :, None] * xs
    y = _rmsnorm(y.reshape(bsz, l, SSM_W) * jax.nn.silu(z_b.astype(f32)), p['ssm_norm_w'])
    br_b = y.astype(x.dtype) @ p['ssm_w_o']
    ssm_s = s_fin.reshape(bsz, SSM_HEADS, SSM_HEADDIM, SSM_STATE)

    val, gat = jnp.split(glu_in, 2, axis=-1)
    u = val * jax.nn.sigmoid(gat)
    u, cfm_buf = _causal_dwconv(u, cfm_buf, p['cfm_conv_w'], p['cfm_conv_b'])
    u = jax.nn.silu(_layernorm(u, p['cfm_ln_w'], p['cfm_ln_b'])) * jax.nn.silu(gate_c)
    br_c = u @ p['cfm_w_o']

    gates = jax.nn.sigmoid(gate_in.astype(f32)).reshape(bsz, l, N_BRANCH, D_MODEL)
    merged = gates[..., 0, :] * br_a + gates[..., 1, :] * br_b + gates[..., 2, :] * br_c
    out = merged.astype(x.dtype) @ p['w_out']
    return x + out, (gdn_s, gdn_buf, ssm_s, ssm_buf, cfm_buf)


def setup_inputs(seed: int = 0) -> dict:
    key = jax.random.key(seed)
    ks = jax.random.split(key, 32)
    f32 = jnp.float32

    def nrm(k, shape, scale):
        return scale * jax.random.normal(k, shape, f32)

    def dt_bias(k, shape):
        dt = jnp.exp(jax.random.uniform(k, shape, f32, math.log(1e-3), math.log(1e-1)))
        return dt + jnp.log(-jnp.expm1(-dt))

    def a_log(k, shape):
        return jnp.log(jax.random.uniform(k, shape, f32, 1.0, 16.0))

    return {
        'x_prompt': nrm(ks[0], (BATCH, SEQ, D_MODEL), 1.0),
        'x_sample': nrm(ks[1], (DEC_BATCH, DEC_SEQ, D_MODEL), 1.0),
        'state_gdn': nrm(ks[2], (DEPTH, DEC_BATCH, GDN_HEADS, GDN_DK, GDN_DV), 0.1),
        'state_gdn_conv': nrm(ks[3], (DEPTH, DEC_BATCH, GDN_CONV - 1, GDN_CONV_CH), 1.0),
        'state_ssm': nrm(ks[4], (DEPTH, DEC_BATCH, SSM_HEADS, SSM_HEADDIM, SSM_STATE), 0.1),
        'state_ssm_conv': nrm(ks[5], (DEPTH, DEC_BATCH, SSM_CONV - 1, SSM_XBC), 1.0),
        'state_cfm_conv': nrm(ks[6], (DEPTH, DEC_BATCH, CFM_KERNEL - 1, CFM_W), 0.5),
        'norm_w': 1.0 + nrm(ks[7], (DEPTH, D_MODEL), 0.02),
        'w_in': nrm(ks[8], (DEPTH, D_MODEL, IN_DIM), D_MODEL ** -0.5),
        'gdn_conv_w': nrm(ks[9], (DEPTH, GDN_CONV, GDN_CONV_CH), GDN_CONV ** -0.5),
        'gdn_a_log': a_log(ks[10], (DEPTH, GDN_HEADS)),
        'gdn_dt_bias': dt_bias(ks[11], (DEPTH, GDN_HEADS)),
        'gdn_norm_w': 1.0 + nrm(ks[12], (DEPTH, GDN_DV), 0.02),
        'gdn_w_o': nrm(ks[13], (DEPTH, GDN_V_W, D_MODEL), GDN_V_W ** -0.5),
        'ssm_conv_w': nrm(ks[14], (DEPTH, SSM_CONV, SSM_XBC), SSM_CONV ** -0.5),
        'ssm_conv_b': nrm(ks[15], (DEPTH, SSM_XBC), 0.02),
        'ssm_a_log': a_log(ks[16], (DEPTH, SSM_HEADS)),
        'ssm_dt_bias': dt_bias(ks[17], (DEPTH, SSM_HEADS)),
        'ssm_d': 1.0 + nrm(ks[18], (DEPTH, SSM_HEADS), 0.1),
        'ssm_norm_w': 1.0 + nrm(ks[19], (DEPTH, SSM_W), 0.02),
        'ssm_w_o': nrm(ks[20], (DEPTH, SSM_W, D_MODEL), SSM_W ** -0.5),
        'cfm_conv_w': nrm(ks[21], (DEPTH, CFM_KERNEL, CFM_W), CFM_KERNEL ** -0.5),
        'cfm_conv_b': nrm(ks[22], (DEPTH, CFM_W), 0.02),
        'cfm_ln_w': 1.0 + nrm(ks[23], (DEPTH, CFM_W), 0.02),
        'cfm_ln_b': nrm(ks[24], (DEPTH, CFM_W), 0.02),
        'cfm_w_o': nrm(ks[25], (DEPTH, CFM_W, D_MODEL), CFM_W ** -0.5),
        'w_out': nrm(ks[26], (DEPTH, D_MODEL, D_MODEL), D_MODEL ** -0.5),
        'final_norm_w': 1.0 + nrm(ks[27], (D_MODEL,), 0.02),
    }


def reference(x_prompt, x_sample, state_gdn, state_gdn_conv, state_ssm, state_ssm_conv, state_cfm_conv,
              norm_w, w_in, gdn_conv_w, gdn_a_log, gdn_dt_bias, gdn_norm_w, gdn_w_o,
              ssm_conv_w, ssm_conv_b, ssm_a_log, ssm_dt_bias, ssm_d, ssm_norm_w, ssm_w_o,
              cfm_conv_w, cfm_conv_b, cfm_ln_w, cfm_ln_b, cfm_w_o, w_out, final_norm_w):
    f32 = jnp.float32
    bp = x_prompt.shape[0]
    hp, hs = x_prompt, x_sample
    p_states, s_states = [], []
    for i in range(DEPTH):
        p = {
            'norm_w': norm_w[i], 'w_in': w_in[i],
            'gdn_conv_w': gdn_conv_w[i], 'gdn_a_log': gdn_a_log[i], 'gdn_dt_bias': gdn_dt_bias[i],
            'gdn_norm_w': gdn_norm_w[i], 'gdn_w_o': gdn_w_o[i],
            'ssm_conv_w': ssm_conv_w[i], 'ssm_conv_b': ssm_conv_b[i], 'ssm_a_log': ssm_a_log[i],
            'ssm_dt_bias': ssm_dt_bias[i], 'ssm_d': ssm_d[i], 'ssm_norm_w': ssm_norm_w[i], 'ssm_w_o': ssm_w_o[i],
            'cfm_conv_w': cfm_conv_w[i], 'cfm_conv_b': cfm_conv_b[i], 'cfm_ln_w': cfm_ln_w[i],
            'cfm_ln_b': cfm_ln_b[i], 'cfm_w_o': cfm_w_o[i], 'w_out': w_out[i],
        }
        zero_st = (jnp.zeros((bp, GDN_HEADS, GDN_DK, GDN_DV), f32),
                   jnp.zeros((bp, GDN_CONV - 1, GDN_CONV_CH), x_prompt.dtype),
                   jnp.zeros((bp, SSM_HEADS, SSM_HEADDIM, SSM_STATE), f32),
                   jnp.zeros((bp, SSM_CONV - 1, SSM_XBC), x_prompt.dtype),
                   jnp.zeros((bp, CFM_KERNEL - 1, CFM_W), x_prompt.dtype))
        hp, stp = _layer(hp, p, zero_st, True)
        past = (state_gdn[i], state_gdn_conv[i], state_ssm[i], state_ssm_conv[i], state_cfm_conv[i])
        hs, sts = _layer(hs, p, past, False)
        p_states.append(stp)
        s_states.append(sts)
    y_prompt = _rmsnorm(hp, final_norm_w)
    y_sample = _rmsnorm(hs, final_norm_w)

    def stack(states, j, dtype):
        return jnp.stack([s[j] for s in states]).astype(dtype)

    pd, sd = x_prompt.dtype, x_sample.dtype
    return (y_prompt, y_sample,
            stack(p_states, 0, pd), stack(p_states, 1, pd), stack(p_states, 2, pd),
            stack(p_states, 3, pd), stack(p_states, 4, pd),
            stack(s_states, 0, sd), stack(s_states, 1, sd), stack(s_states, 2, sd),
            stack(s_states, 3, sd), stack(s_states, 4, sd))
```

```python
import functools

import jax
import jax.numpy as jnp
from jax import lax
from jax.experimental import pallas as pl
from jax.experimental.pallas import tpu as pltpu

F32 = jnp.float32
BF16 = jnp.bfloat16
HIGHEST = lax.Precision.HIGHEST

D_MODEL = 1024
GDN_HEADS = 4
GDN_D = 128
GDN_QK_W = GDN_HEADS * GDN_D
GDN_CH = 3 * GDN_QK_W
SSM_W = 1024
SSM_P = 64
SSM_HEADS = 16
SSM_GROUPS = 2
SSM_RPG = SSM_HEADS // SSM_GROUPS
SSM_GW = SSM_RPG * SSM_P
SSM_N = 128
SSM_XBC = SSM_W + 2 * SSM_GROUPS * SSM_N
CFM_W = 512
CFM_K = 31
CFM_HIST = 32
CONV_HIST = 8
CHUNK = 64
EPS = 1e-6
NEG = -1e30
LANES = 128
SMALL_W = 3 * LANES
SMALL_T_ROWS = 32

MAIN_W = 9216
COL_QKV, COL_XBC, COL_ZB, COL_GLU, COL_ZA, COL_GC, COL_GATES = 0, 1536, 3072, 4096, 5120, 5632, 6144

VMEM_LIMIT = 56 * 1024 * 1024


def _sds(shape, dtype):
    return jax.ShapeDtypeStruct(shape, dtype)


def _params(n_axes):
    return pltpu.CompilerParams(dimension_semantics=("arbitrary",) * n_axes,
                                vmem_limit_bytes=VMEM_LIMIT)


def _sigmoid(x):
    return jax.nn.sigmoid(x)


def _silu(x):
    return x * jax.nn.sigmoid(x)


def _softplus(x):
    return jnp.maximum(x, 0.0) + jnp.log1p(jnp.exp(-jnp.abs(x)))


def _mm(a, b):
    return jnp.dot(a.astype(BF16), b.astype(BF16), preferred_element_type=F32)


def _mm_nt(a, b):
    return lax.dot_general(a.astype(BF16), b.astype(BF16), (((1,), (1,)), ((), ())),
                           preferred_element_type=F32)


def _mm_tn(a, b):
    return lax.dot_general(a.astype(BF16), b.astype(BF16), (((0,), (0,)), ((), ())),
                           preferred_element_type=F32)


def _mm_exact(a, b):
    return jnp.dot(a, b, precision=HIGHEST, preferred_element_type=F32)


def _rms(x, w):
    return x * lax.rsqrt(jnp.mean(x * x, axis=-1, keepdims=True) + EPS) * w


def _tri_masks():
    r = lax.broadcasted_iota(jnp.int32, (CHUNK, CHUNK), 0)
    c = lax.broadcasted_iota(jnp.int32, (CHUNK, CHUNK), 1)
    return r >= c, r > c, r <= c


def _expand_heads(x, emat):
    hi = x.astype(BF16)
    r1 = x - hi.astype(F32)
    mid = r1.astype(BF16)
    lo = (r1 - mid.astype(F32)).astype(BF16)
    dot = functools.partial(jnp.dot, preferred_element_type=F32)
    return dot(hi, emat) + dot(mid, emat) + dot(lo, emat)


def _expand_matrix():
    r = lax.broadcasted_iota(jnp.int32, (LANES, SSM_W), 0)
    c = lax.broadcasted_iota(jnp.int32, (LANES, SSM_W), 1)
    return jnp.where((c >> 6) == r, 1.0, 0.0).astype(BF16)


def _causal_conv(x, hist, w_ref, taps):
    t_len = x.shape[0]
    h_len = hist.shape[0]
    xe = jnp.concatenate([hist, x], axis=0)
    acc = x * w_ref[taps - 1:taps, :]
    rolled = {}
    for s in range(1, taps):
        a, b = divmod(s, 8)
        if b == 0:
            src = xe
        else:
            if b not in rolled:
                rolled[b] = pltpu.roll(xe, b, axis=0)
            src = rolled[b]
        lo = h_len - 8 * a
        acc = acc + src[lo:lo + t_len, :] * w_ref[taps - 1 - s:taps - s, :]
    return acc, xe[t_len:, :]


def _inproj_kernel(x_ref, nw_ref, w_ref, ws_ref, wst_ref, o_ref, os_ref, ost_ref, h_ref):
    @pl.when(pl.program_id(1) == 0)
    def _():
        h = _rms(x_ref[...], nw_ref[...]).astype(BF16)
        h_ref[...] = h
        os_ref[...] = jnp.dot(h, ws_ref[...], preferred_element_type=F32)
        ost_ref[...] = lax.dot_general(wst_ref[...], h, (((1,), (1,)), ((), ())),
                                       preferred_element_type=F32)

    o_ref[...] = jnp.dot(h_ref[...], w_ref[...], preferred_element_type=F32)


def _inproj(x, norm_w, w_main, w_small, w_small_t, tm):
    m = x.shape[0]
    tn = 1024
    return pl.pallas_call(
        _inproj_kernel,
        grid=(m // tm, MAIN_W // tn),
        in_specs=[
            pl.BlockSpec((tm, D_MODEL), lambda i, j: (i, 0)),
            pl.BlockSpec((1, D_MODEL), lambda i, j: (0, 0)),
            pl.BlockSpec((D_MODEL, tn), lambda i, j: (0, j)),
            pl.BlockSpec((D_MODEL, SMALL_W), lambda i, j: (0, 0)),
            pl.BlockSpec((SMALL_T_ROWS, D_MODEL), lambda i, j: (0, 0)),
        ],
        out_specs=[
            pl.BlockSpec((tm, tn), lambda i, j: (i, j)),
            pl.BlockSpec((tm, SMALL_W), lambda i, j: (i, 0)),
            pl.BlockSpec((SMALL_T_ROWS, tm), lambda i, j: (0, i)),
        ],
        out_shape=[_sds((m, MAIN_W), F32), _sds((m, SMALL_W), F32), _sds((SMALL_T_ROWS, m), F32)],
        scratch_shapes=[pltpu.VMEM((tm, D_MODEL), BF16)],
        compiler_params=_params(2),
        name="inproj",
    )(x, norm_w, w_main, w_small, w_small_t)


def _unit_lower_inverse(n):
    r = lax.broadcasted_iota(jnp.int32, (CHUNK, CHUNK), 0)
    c = lax.broadcasted_iota(jnp.int32, (CHUNK, CHUNK), 1)
    eye = jnp.where(r == c, 1.0, 0.0).astype(F32)
    p = eye - n
    pw = _mm(n, n)
    for _ in range(4):
        p = p + _mm(p, pw)
        pw = _mm(pw, pw)
    return p + _mm(p, pw)


def _gdn_prompt_kernel(qkv_ref, za_ref, sm_ref, smt_ref, cw_ref, pc_ref, pr_ref, nw_ref,
                       o_ref, st_ref, cv_ref, s_scr, hist_scr, o_scr, *, t_len):
    @pl.when(pl.program_id(1) == 0)
    def _():
        s_scr[...] = jnp.zeros_like(s_scr)
        hist_scr[...] = jnp.zeros_like(hist_scr)

    x = qkv_ref[...]
    y, new_hist = _causal_conv(x, hist_scr[...], cw_ref, 4)
    hist_scr[...] = new_hist
    cv_ref[0] = new_hist
    qkv = _silu(y)

    sm = sm_ref[...]
    beta_c = _sigmoid(sm[:, 0:LANES])
    g_c = -jnp.exp(pc_ref[0:1, :]) * _softplus(sm[:, LANES:2 * LANES] + pc_ref[1:2, :])
    nrep = t_len // LANES
    g_r = (-jnp.exp(jnp.tile(pr_ref[0:8, :], (1, nrep)))
           * _softplus(smt_ref[8:16, :] + jnp.tile(pr_ref[8:16, :], (1, nrep))))

    qs, ks, vs = [], [], []
    for h in range(GDN_HEADS):
        q = qkv[:, h * GDN_D:(h + 1) * GDN_D]
        k = qkv[:, GDN_QK_W + h * GDN_D:GDN_QK_W + (h + 1) * GDN_D]
        qs.append(q * lax.rsqrt(jnp.sum(q * q, axis=-1, keepdims=True) + EPS) * (GDN_D ** -0.5))
        ks.append(k * lax.rsqrt(jnp.sum(k * k, axis=-1, keepdims=True) + EPS))
        vs.append(qkv[:, 2 * GDN_QK_W + h * GDN_D:2 * GDN_QK_W + (h + 1) * GDN_D])

    incl, strict, upper = _tri_masks()
    ltri = jnp.where(incl, 1.0, 0.0).astype(F32)
    utri = jnp.where(upper, 1.0, 0.0).astype(F32)

    for c in range(t_len // CHUNK):
        r0 = c * CHUNK
        gam_c = _mm_exact(ltri, g_c[r0:r0 + CHUNK, :])
        gam_r = _mm_exact(g_r[:, r0:r0 + CHUNK], utri)
        for h in range(GDN_HEADS):
            gcol = gam_c[:, h:h + 1]
            grow = gam_r[h:h + 1, :]
            glast = gam_c[CHUNK - 1:CHUNK, h:h + 1]
            dec_incl = jnp.exp(jnp.where(incl, gcol - grow, NEG))
            dec_strict = jnp.where(strict, dec_incl, 0.0)
            bcol = beta_c[r0:r0 + CHUNK, h:h + 1]
            q = qs[h][r0:r0 + CHUNK, :]
            k = ks[h][r0:r0 + CHUNK, :]
            v = vs[h][r0:r0 + CHUNK, :]
            eg = jnp.exp(gcol)
            kb = k * bcol
            tinv = _unit_lower_inverse(_mm_nt(kb, k) * dec_strict)
            wu = _mm(tinv, jnp.concatenate([kb * eg, v * bcol], axis=1))
            w_c, u_c = wu[:, :GDN_D], wu[:, GDN_D:]
            attn = _mm_nt(q, k) * dec_incl
            s = s_scr[h]
            u = u_c - _mm(w_c, s)
            o_scr[r0:r0 + CHUNK, h * GDN_D:(h + 1) * GDN_D] = _mm(q * eg, s) + _mm(attn, u)
            s_scr[h] = s * jnp.exp(glast) + _mm_tn(k * jnp.exp(glast - gcol), u)

    o = o_scr[...]
    za = za_ref[...]
    outs = []
    for h in range(GDN_HEADS):
        sl = slice(h * GDN_D, (h + 1) * GDN_D)
        outs.append(_rms(o[:, sl], nw_ref[...]) * _silu(za[:, sl]))
    o_ref[...] = jnp.concatenate(outs, axis=1).astype(BF16)
    st_ref[0] = s_scr[...]


def _gdn_prompt(proj, sm, smt, conv_w, pc, pr, norm_w, batch, seq, t_len):
    nt = seq // t_len
    row = lambda b, t: b * nt + t
    kern = functools.partial(_gdn_prompt_kernel, t_len=t_len)
    const = lambda shape: pl.BlockSpec(shape, lambda b, t: (0,) * len(shape))
    return pl.pallas_call(
        kern,
        grid=(batch, nt),
        in_specs=[
            pl.BlockSpec((t_len, GDN_CH), lambda b, t: (row(b, t), COL_QKV // GDN_CH)),
            pl.BlockSpec((t_len, GDN_QK_W), lambda b, t: (row(b, t), COL_ZA // GDN_QK_W)),
            pl.BlockSpec((t_len, SMALL_W), lambda b, t: (row(b, t), 0)),
            pl.BlockSpec((SMALL_T_ROWS, t_len), lambda b, t: (0, row(b, t))),
            const((4, GDN_CH)), const((8, LANES)), const((16, LANES)), const((1, GDN_D)),
        ],
        out_specs=[
            pl.BlockSpec((t_len, GDN_QK_W), lambda b, t: (row(b, t), 0)),
            pl.BlockSpec((1, GDN_HEADS, GDN_D, GDN_D), lambda b, t: (b, 0, 0, 0)),
            pl.BlockSpec((1, CONV_HIST, GDN_CH), lambda b, t: (b, 0, 0)),
        ],
        out_shape=[_sds((batch * seq, GDN_QK_W), BF16),
                   _sds((batch, GDN_HEADS, GDN_D, GDN_D), F32),
                   _sds((batch, CONV_HIST, GDN_CH), F32)],
        scratch_shapes=[pltpu.VMEM((GDN_HEADS, GDN_D, GDN_D), F32),
                        pltpu.VMEM((CONV_HIST, GDN_CH), F32),
                        pltpu.VMEM((t_len, GDN_QK_W), F32)],
        compiler_params=_params(2),
        name="gdn_prompt",
    )(proj, proj, sm, smt, conv_w, pc, pr, norm_w)


def _ssd_prompt_kernel(xbc_ref, zb_ref, sm_ref, smt_ref, cw_ref, cb_ref, pc_ref, pr_ref, d_ref, nw_ref,
                       o_ref, st_ref, cv_ref, st_scr, hist_scr, y_scr, *, t_len):
    t_idx = pl.program_id(1)

    @pl.when(t_idx == 0)
    def _():
        st_scr[...] = jnp.zeros_like(st_scr)
        hist_scr[...] = jnp.zeros_like(hist_scr)

    x = xbc_ref[...]
    y, new_hist = _causal_conv(x, hist_scr[...], cw_ref, 4)
    hist_scr[...] = new_hist
    cv_ref[0] = new_hist
    xbc = _silu(y + cb_ref[...])
    xs = xbc[:, :SSM_W]
    bm = xbc[:, SSM_W:SSM_W + SSM_GROUPS * SSM_N]
    cm = xbc[:, SSM_W + SSM_GROUPS * SSM_N:]

    lane = lax.broadcasted_iota(jnp.int32, (1, LANES), 1)
    dt_c = jnp.where(lane < SSM_HEADS,
                     _softplus(sm_ref[:, 2 * LANES:3 * LANES] + pc_ref[1:2, :]), 0.0)
    ga_c = dt_c * (-jnp.exp(pc_ref[0:1, :]))
    nrep = t_len // LANES
    dt_r = _softplus(smt_ref[16:32, :] + jnp.tile(pr_ref[16:32, :], (1, nrep)))
    ga_r = dt_r * (-jnp.exp(jnp.tile(pr_ref[0:16, :], (1, nrep))))

    incl, _, upper = _tri_masks()
    ltri = jnp.where(incl, 1.0, 0.0).astype(F32)
    utri = jnp.where(upper, 1.0, 0.0).astype(F32)
    emat = _expand_matrix()
    low_half = lane < SSM_P

    for c in range(t_len // CHUNK):
        r0 = c * CHUNK
        gam_c = _mm_exact(ltri, ga_c[r0:r0 + CHUNK, :])
        gam_r = _mm_exact(ga_r[:, r0:r0 + CHUNK], utri)
        dtr = dt_r[:, r0:r0 + CHUNK]
        glast = gam_c[CHUNK - 1:CHUNK, :]
        e1 = _expand_heads(jnp.exp(gam_c), emat)
        e2 = _expand_heads(jnp.exp(glast - gam_c) * dt_c[r0:r0 + CHUNK, :], emat)
        dtot = e1[CHUNK - 1:CHUNK, :]
        xs_c = xs[r0:r0 + CHUNK, :]
        bm_c = bm[r0:r0 + CHUNK, :]
        cm_c = cm[r0:r0 + CHUNK, :]
        cbs = [_mm_nt(cm_c[:, g * SSM_N:(g + 1) * SSM_N], bm_c[:, g * SSM_N:(g + 1) * SSM_N])
               for g in range(SSM_GROUPS)]
        pairs = []
        for j in range(SSM_HEADS // 2):
            xp = xs_c[:, j * LANES:(j + 1) * LANES]
            acc = None
            for half in range(2):
                h = 2 * j + half
                m = (cbs[h // SSM_RPG]
                     * jnp.exp(jnp.where(incl, gam_c[:, h:h + 1] - gam_r[h:h + 1, :], NEG))
                     * dtr[h:h + 1, :])
                xm = jnp.where(low_half if half == 0 else jnp.logical_not(low_half), xp, 0.0)
                part = _mm(m, xm)
                acc = part if acc is None else acc + part
            pairs.append(acc)
        y_intra = jnp.concatenate(pairs, axis=1)
        y_inter = jnp.concatenate(
            [_mm(cm_c[:, g * SSM_N:(g + 1) * SSM_N], st_scr[g]) for g in range(SSM_GROUPS)], axis=1)
        y_scr[r0:r0 + CHUNK, :] = y_intra + y_inter * e1 + d_ref[...] * xs_c
        xe = xs_c * e2
        for g in range(SSM_GROUPS):
            sl = slice(g * SSM_GW, (g + 1) * SSM_GW)
            st_scr[g] = st_scr[g] * dtot[:, sl] + _mm_tn(bm_c[:, g * SSM_N:(g + 1) * SSM_N], xe[:, sl])

    o_ref[...] = _rms(y_scr[...] * _silu(zb_ref[...]), nw_ref[...]).astype(BF16)

    @pl.when(t_idx == pl.num_programs(1) - 1)
    def _():
        for g in range(SSM_GROUPS):
            st_ref[0, g * SSM_RPG:(g + 1) * SSM_RPG] = st_scr[g].T.reshape(SSM_RPG, SSM_P, SSM_N)


def _ssd_prompt(proj, sm, smt, conv_w, conv_b, pc, pr, d_row, norm_w, batch, seq, t_len):
    nt = seq // t_len
    row = lambda b, t: b * nt + t
    kern = functools.partial(_ssd_prompt_kernel, t_len=t_len)
    const = lambda shape: pl.BlockSpec(shape, lambda b, t: (0,) * len(shape))
    return pl.pallas_call(
        kern,
        grid=(batch, nt),
        in_specs=[
            pl.BlockSpec((t_len, SSM_XBC), lambda b, t: (row(b, t), COL_XBC // SSM_XBC)),
            pl.BlockSpec((t_len, SSM_W), lambda b, t: (row(b, t), COL_ZB // SSM_W)),
            pl.BlockSpec((t_len, SMALL_W), lambda b, t: (row(b, t), 0)),
            pl.BlockSpec((SMALL_T_ROWS, t_len), lambda b, t: (0, row(b, t))),
            const((4, SSM_XBC)), const((1, SSM_XBC)), const((8, LANES)), const((32, LANES)),
            const((1, SSM_W)), const((1, SSM_W)),
        ],
        out_specs=[
            pl.BlockSpec((t_len, SSM_W), lambda b, t: (row(b, t), 0)),
            pl.BlockSpec((1, SSM_HEADS, SSM_P, SSM_N), lambda b, t: (b, 0, 0, 0)),
            pl.BlockSpec((1, CONV_HIST, SSM_XBC), lambda b, t: (b, 0, 0)),
        ],
        out_shape=[_sds((batch * seq, SSM_W), BF16),
                   _sds((batch, SSM_HEADS, SSM_P, SSM_N), F32),
                   _sds((batch, CONV_HIST, SSM_XBC), F32)],
        scratch_shapes=[pltpu.VMEM((SSM_GROUPS, SSM_N, SSM_GW), F32),
                        pltpu.VMEM((CONV_HIST, SSM_XBC), F32),
                        pltpu.VMEM((t_len, SSM_W), F32)],
        compiler_params=_params(2),
        name="ssd_prompt",
    )(proj, proj, sm, smt, conv_w, conv_b, pc, pr, d_row, norm_w)


def _layernorm_silu_gate(u, gate, lnw, lnb):
    mu = jnp.mean(u, axis=-1, keepdims=True)
    uc = u - mu
    var = jnp.mean(uc * uc, axis=-1, keepdims=True)
    return _silu(uc * lax.rsqrt(var + EPS) * lnw + lnb) * _silu(gate)


def _cfm_prompt_kernel(glu_ref, gc_ref, cw_ref, cb_ref, lnw_ref, lnb_ref, o_ref, cv_ref, hist_scr):
    @pl.when(pl.program_id(1) == 0)
    def _():
        hist_scr[...] = jnp.zeros_like(hist_scr)

    glu = glu_ref[...]
    u0 = glu[:, :CFM_W] * _sigmoid(glu[:, CFM_W:])
    u, new_hist = _causal_conv(u0, hist_scr[...], cw_ref, CFM_K)
    hist_scr[...] = new_hist
    cv_ref[0] = new_hist
    o_ref[...] = _layernorm_silu_gate(u + cb_ref[...], gc_ref[...], lnw_ref[...], lnb_ref[...]).astype(BF16)


def _cfm_prompt(proj, conv_w, conv_b, ln_w, ln_b, batch, seq, t_len):
    nt = seq // t_len
    row = lambda b, t: b * nt + t
    const = lambda shape: pl.BlockSpec(shape, lambda b, t: (0,) * len(shape))
    return pl.pallas_call(
        _cfm_prompt_kernel,
        grid=(batch, nt),
        in_specs=[
            pl.BlockSpec((t_len, 2 * CFM_W), lambda b, t: (row(b, t), COL_GLU // (2 * CFM_W))),
            pl.BlockSpec((t_len, CFM_W), lambda b, t: (row(b, t), COL_GC // CFM_W)),
            const((CFM_K, CFM_W)), const((1, CFM_W)), const((1, CFM_W)), const((1, CFM_W)),
        ],
        out_specs=[
            pl.BlockSpec((t_len, CFM_W), lambda b, t: (row(b, t), 0)),
            pl.BlockSpec((1, CFM_HIST, CFM_W), lambda b, t: (b, 0, 0)),
        ],
        out_shape=[_sds((batch * seq, CFM_W), BF16), _sds((batch, CFM_HIST, CFM_W), F32)],
        scratch_shapes=[pltpu.VMEM((CFM_HIST, CFM_W), F32)],
        compiler_params=_params(2),
        name="cfm_prompt",
    )(proj, proj, conv_w, conv_b, ln_w, ln_b)


def _out_kernel(x_ref, oa_ref, yb_ref, uc_ref, g_ref, wa_ref, wb_ref, wc_ref, wo_ref, fw_ref, o_ref, *, final):
    dot = functools.partial(jnp.dot, preferred_element_type=F32)
    g = g_ref[...]
    merged = (_sigmoid(g[:, :D_MODEL]) * dot(oa_ref[...], wa_ref[...])
              + _sigmoid(g[:, D_MODEL:2 * D_MODEL]) * dot(yb_ref[...], wb_ref[...])
              + _sigmoid(g[:, 2 * D_MODEL:]) * dot(uc_ref[...], wc_ref[...]))
    out = x_ref[...] + dot(merged.astype(BF16), wo_ref[...])
    if final:
        out = _rms(out, fw_ref[...])
    o_ref[...] = out


def _out_proj(x, oa, yb, uc, proj, wa, wb, wc, wo, final_w, tm, final):
    m = x.shape[0]
    const = lambda shape: pl.BlockSpec(shape, lambda i: (0,) * len(shape))
    return pl.pallas_call(
        functools.partial(_out_kernel, final=final),
        grid=(m // tm,),
        in_specs=[
            pl.BlockSpec((tm, D_MODEL), lambda i: (i, 0)),
            pl.BlockSpec((tm, GDN_QK_W), lambda i: (i, 0)),
            pl.BlockSpec((tm, SSM_W), lambda i: (i, 0)),
            pl.BlockSpec((tm, CFM_W), lambda i: (i, 0)),
            pl.BlockSpec((tm, 3 * D_MODEL), lambda i: (i, COL_GATES // (3 * D_MODEL))),
            const((GDN_QK_W, D_MODEL)), const((SSM_W, D_MODEL)), const((CFM_W, D_MODEL)),
            const((D_MODEL, D_MODEL)), const((1, D_MODEL)),
        ],
        out_specs=pl.BlockSpec((tm, D_MODEL), lambda i: (i, 0)),
        out_shape=_sds((m, D_MODEL), F32),
        compiler_params=_params(1),
        name="merge_out",
    )(x, oa, yb, uc, proj, wa, wb, wc, wo, final_w)


TOK_GROUP = 8


def _gdn_sample_kernel(qkv_ref, za_ref, sm_ref, buf_ref, cw_ref, pc_ref, nw_ref, s_ref,
                       o_ref, so_ref, bo_ref,
                       kt_scr, qt_scr, v_scr, a_scr, b_scr, qk_scr, o_scr):
    tg = pl.program_id(0)

    @pl.when(tg == 0)
    def _():
        x = qkv_ref[...]
        y = (cw_ref[0:1, :] * buf_ref[0] + cw_ref[1:2, :] * buf_ref[1]
             + cw_ref[2:3, :] * buf_ref[2] + cw_ref[3:4, :] * x)
        bo_ref[0] = buf_ref[1]
        bo_ref[1] = buf_ref[2]
        bo_ref[2] = x
        qkv = _silu(y)
        sm = sm_ref[...]
        beta = _sigmoid(sm[:, 0:LANES])
        decay = jnp.exp(-jnp.exp(pc_ref[0:1, :]) * _softplus(sm[:, LANES:2 * LANES] + pc_ref[1:2, :]))
        n_tok = x.shape[0]
        for h in range(GDN_HEADS):
            q = qkv[:, h * GDN_D:(h + 1) * GDN_D]
            k = qkv[:, GDN_QK_W + h * GDN_D:GDN_QK_W + (h + 1) * GDN_D]
            q = q * lax.rsqrt(jnp.sum(q * q, axis=-1, keepdims=True) + EPS) * (GDN_D ** -0.5)
            k = k * lax.rsqrt(jnp.sum(k * k, axis=-1, keepdims=True) + EPS)
            kt_scr[h] = k.T
            qt_scr[h] = q.T
            a_scr[h] = jnp.broadcast_to(decay[:, h:h + 1], (n_tok, LANES))
            b_scr[h] = jnp.broadcast_to(beta[:, h:h + 1], (n_tok, LANES))
            qk_scr[h] = jnp.broadcast_to(jnp.sum(q * k, axis=-1, keepdims=True), (n_tok, LANES))
        v_scr[...] = qkv[:, 2 * GDN_QK_W:]

    shift = (LANES - TOK_GROUP * tg) & (LANES - 1)
    rows = pl.ds(pl.multiple_of(tg * TOK_GROUP, TOK_GROUP), TOK_GROUP)
    for h in range(GDN_HEADS):
        ktg = pltpu.roll(kt_scr[h], shift, axis=1)
        qtg = pltpu.roll(qt_scr[h], shift, axis=1)
        a_blk = a_scr[h, rows, :]
        b_blk = b_scr[h, rows, :]
        qk_blk = qk_scr[h, rows, :]
        v_blk = v_scr[rows, h * GDN_D:(h + 1) * GDN_D]
        o_rows = []
        for j in range(TOK_GROUP):
            s = s_ref[j, h]
            kcol = ktg[:, j:j + 1]
            qcol = qtg[:, j:j + 1]
            ks = jnp.sum(s * kcol, axis=0, keepdims=True)
            qs = jnp.sum(s * qcol, axis=0, keepdims=True)
            a_row = a_blk[j:j + 1, :]
            delta = b_blk[j:j + 1, :] * (v_blk[j:j + 1, :] - a_row * ks)
            so_ref[j, h] = a_row * s + kcol * delta
            o_rows.append(a_row * qs + qk_blk[j:j + 1, :] * delta)
        o_scr[rows, h * GDN_D:(h + 1) * GDN_D] = jnp.concatenate(o_rows, axis=0)

    @pl.when(tg == pl.num_programs(0) - 1)
    def _():
        o = o_scr[...]
        za = za_ref[...]
        outs = []
        for h in range(GDN_HEADS):
            sl = slice(h * GDN_D, (h + 1) * GDN_D)
            outs.append(_rms(o[:, sl], nw_ref[...]) * _silu(za[:, sl]))
        o_ref[...] = jnp.concatenate(outs, axis=1).astype(BF16)


def _gdn_sample(proj, sm, buf_t, conv_w, pc, norm_w, state):
    n_tok = proj.shape[0]
    const = lambda shape: pl.BlockSpec(shape, lambda g: (0,) * len(shape))
    st_spec = pl.BlockSpec((TOK_GROUP, GDN_HEADS, GDN_D, GDN_D), lambda g: (g, 0, 0, 0))
    return pl.pallas_call(
        _gdn_sample_kernel,
        grid=(n_tok // TOK_GROUP,),
        in_specs=[
            pl.BlockSpec((n_tok, GDN_CH), lambda g: (0, COL_QKV // GDN_CH)),
            pl.BlockSpec((n_tok, GDN_QK_W), lambda g: (0, COL_ZA // GDN_QK_W)),
            const((n_tok, SMALL_W)), const((3, n_tok, GDN_CH)), const((4, GDN_CH)),
            const((8, LANES)), const((1, GDN_D)), st_spec,
        ],
        out_specs=[const((n_tok, GDN_QK_W)), st_spec, const((3, n_tok, GDN_CH))],
        out_shape=[_sds((n_tok, GDN_QK_W), BF16), _sds(state.shape, F32), _sds((3, n_tok, GDN_CH), F32)],
        scratch_shapes=[pltpu.VMEM((GDN_HEADS, GDN_D, n_tok), F32),
                        pltpu.VMEM((GDN_HEADS, GDN_D, n_tok), F32),
                        pltpu.VMEM((n_tok, GDN_QK_W), F32),
                        pltpu.VMEM((GDN_HEADS, n_tok, LANES), F32),
                        pltpu.VMEM((GDN_HEADS, n_tok, LANES), F32),
                        pltpu.VMEM((GDN_HEADS, n_tok, LANES), F32),
                        pltpu.VMEM((n_tok, GDN_QK_W), F32)],
        compiler_params=_params(1),
        name="gdn_sample",
    )(proj, proj, sm, buf_t, conv_w, pc, norm_w, state)


def _ssd_sample_kernel(xbc_ref, zb_ref, sm_ref, buf_ref, cw_ref, cb_ref, pc_ref, d_ref, nw_ref, s_ref,
                       o_ref, so_ref, bo_ref,
                       xt_scr, xs_scr, bm_scr, cm_scr, a_scr, yt_scr):
    tg = pl.program_id(0)
    n_tok = xbc_ref.shape[0]

    @pl.when(tg == 0)
    def _():
        x = xbc_ref[...]
        y = (cw_ref[0:1, :] * buf_ref[0] + cw_ref[1:2, :] * buf_ref[1]
             + cw_ref[2:3, :] * buf_ref[2] + cw_ref[3:4, :] * x)
        bo_ref[0] = buf_ref[1]
        bo_ref[1] = buf_ref[2]
        bo_ref[2] = x
        xbc = _silu(y + cb_ref[...])
        xs = xbc[:, :SSM_W]
        xs_scr[...] = xs
        bm_scr[...] = xbc[:, SSM_W:SSM_W + SSM_GROUPS * SSM_N]
        cm_scr[...] = xbc[:, SSM_W + SSM_GROUPS * SSM_N:]
        lane = lax.broadcasted_iota(jnp.int32, (1, LANES), 1)
        dt = jnp.where(lane < SSM_HEADS,
                       _softplus(sm_ref[:, 2 * LANES:3 * LANES] + pc_ref[1:2, :]), 0.0)
        decay = jnp.exp(dt * (-jnp.exp(pc_ref[0:1, :])))
        xt_scr[...] = (xs * _expand_heads(dt, _expand_matrix())).T
        for h in range(SSM_HEADS):
            a_scr[h] = jnp.broadcast_to(decay[:, h:h + 1], (n_tok, LANES))
        yt_scr[...] = jnp.zeros_like(yt_scr)

    shift = (LANES - TOK_GROUP * tg) & (LANES - 1)
    rows = pl.ds(pl.multiple_of(tg * TOK_GROUP, TOK_GROUP), TOK_GROUP)
    xtg = pltpu.roll(xt_scr[...], shift, axis=1)
    bm_blk = bm_scr[rows, :]
    cm_blk = cm_scr[rows, :]
    lane = lax.broadcasted_iota(jnp.int32, (1, LANES), 1)
    y_tile = jnp.zeros((SSM_W, LANES), F32)
    for j in range(TOK_GROUP):
        y_cols = []
        for h in range(SSM_HEADS):
            g = h // SSM_RPG
            s = s_ref[j, h]
            a_row = a_scr[h, rows, :][j:j + 1, :]
            xcol = xtg[h * SSM_P:(h + 1) * SSM_P, j:j + 1]
            s_new = a_row * s + xcol * bm_blk[j:j + 1, g * SSM_N:(g + 1) * SSM_N]
            so_ref[j, h] = s_new
            y_cols.append(jnp.sum(s_new * cm_blk[j:j + 1, g * SSM_N:(g + 1) * SSM_N], axis=1, keepdims=True))
        y_tile = jnp.where(lane == j, jnp.concatenate(y_cols, axis=0), y_tile)
    in_group = (lane >> 3) == tg
    yt_scr[...] = jnp.where(in_group, pltpu.roll(y_tile, TOK_GROUP * tg, axis=1), yt_scr[...])

    @pl.when(tg == pl.num_programs(0) - 1)
    def _():
        xs = xs_scr[...]
        y = yt_scr[...].T + d_ref[...] * xs
        o_ref[...] = _rms(y * _silu(zb_ref[...]), nw_ref[...]).astype(BF16)


def _ssd_sample(proj, sm, buf_t, conv_w, conv_b, pc, d_row, norm_w, state):
    n_tok = proj.shape[0]
    const = lambda shape: pl.BlockSpec(shape, lambda g: (0,) * len(shape))
    st_spec = pl.BlockSpec((TOK_GROUP, SSM_HEADS, SSM_P, SSM_N), lambda g: (g, 0, 0, 0))
    return pl.pallas_call(
        _ssd_sample_kernel,
        grid=(n_tok // TOK_GROUP,),
        in_specs=[
            pl.BlockSpec((n_tok, SSM_XBC), lambda g: (0, COL_XBC // SSM_XBC)),
            pl.BlockSpec((n_tok, SSM_W), lambda g: (0, COL_ZB // SSM_W)),
            const((n_tok, SMALL_W)), const((3, n_tok, SSM_XBC)), const((4, SSM_XBC)), const((1, SSM_XBC)),
            const((8, LANES)), const((1, SSM_W)), const((1, SSM_W)), st_spec,
        ],
        out_specs=[const((n_tok, SSM_W)), st_spec, const((3, n_tok, SSM_XBC))],
        out_shape=[_sds((n_tok, SSM_W), BF16), _sds(state.shape, F32), _sds((3, n_tok, SSM_XBC), F32)],
        scratch_shapes=[pltpu.VMEM((SSM_W, n_tok), F32),
                        pltpu.VMEM((n_tok, SSM_W), F32),
                        pltpu.VMEM((n_tok, SSM_GROUPS * SSM_N), F32),
                        pltpu.VMEM((n_tok, SSM_GROUPS * SSM_N), F32),
                        pltpu.VMEM((SSM_HEADS, n_tok, LANES), F32),
                        pltpu.VMEM((SSM_W, n_tok), F32)],
        compiler_params=_params(1),
        name="ssd_sample",
    )(proj, proj, sm, buf_t, conv_w, conv_b, pc, d_row, norm_w, state)


def _cfm_sample_kernel(glu_ref, gc_ref, buf_ref, cw_ref, cb_ref, lnw_ref, lnb_ref, o_ref, bo_ref):
    glu = glu_ref[...]
    u0 = glu[:, :CFM_W] * _sigmoid(glu[:, CFM_W:])
    acc = cw_ref[CFM_K - 1:CFM_K, :] * u0 + cb_ref[...]
    for j in range(CFM_K - 1):
        row = buf_ref[j]
        acc = acc + cw_ref[j:j + 1, :] * row
        if j > 0:
            bo_ref[j - 1] = row
    bo_ref[CFM_K - 2] = u0
    o_ref[...] = _layernorm_silu_gate(acc, gc_ref[...], lnw_ref[...], lnb_ref[...]).astype(BF16)


def _cfm_sample(proj, buf_t, conv_w, conv_b, ln_w, ln_b):
    n_tok = proj.shape[0]
    const = lambda shape: pl.BlockSpec(shape, lambda i: (0,) * len(shape))
    return pl.pallas_call(
        _cfm_sample_kernel,
        grid=(1,),
        in_specs=[
            pl.BlockSpec((n_tok, 2 * CFM_W), lambda i: (0, COL_GLU // (2 * CFM_W))),
            pl.BlockSpec((n_tok, CFM_W), lambda i: (0, COL_GC // CFM_W)),
            const((CFM_K - 1, n_tok, CFM_W)), const((CFM_K, CFM_W)),
            const((1, CFM_W)), const((1, CFM_W)), const((1, CFM_W)),
        ],
        out_specs=[const((n_tok, CFM_W)), const((CFM_K - 1, n_tok, CFM_W))],
        out_shape=[_sds((n_tok, CFM_W), BF16), _sds((CFM_K - 1, n_tok, CFM_W), F32)],
        compiler_params=_params(1),
        name="cfm_sample",
    )(proj, proj, buf_t, conv_w, conv_b, ln_w, ln_b)


_IN_Q, _IN_ZA, _IN_BA, _IN_AA, _IN_ZB, _IN_XBC, _IN_DT, _IN_GLU, _IN_GC, _IN_GATES, _IN_END = (
    0, 1536, 2048, 2052, 2056, 3080, 4616, 4632, 5656, 6168, 9240)


def _prep_w_in(w_in):
    seg = lambda a, b: w_in[:, :, a:b]
    zeros = lambda n: jnp.zeros(w_in.shape[:2] + (n,), w_in.dtype)
    main = jnp.concatenate([seg(_IN_Q, _IN_ZA), seg(_IN_XBC, _IN_DT), seg(_IN_ZB, _IN_XBC),
                            seg(_IN_GLU, _IN_GC), seg(_IN_ZA, _IN_BA), seg(_IN_GC, _IN_GATES),
                            seg(_IN_GATES, _IN_END)], axis=-1).astype(BF16)
    small = jnp.concatenate([seg(_IN_BA, _IN_AA), zeros(LANES - GDN_HEADS),
                             seg(_IN_AA, _IN_ZB), zeros(LANES - GDN_HEADS),
                             seg(_IN_DT, _IN_GLU), zeros(LANES - SSM_HEADS)], axis=-1).astype(BF16)
    small_t = jnp.concatenate([seg(_IN_BA, _IN_AA), zeros(8 - GDN_HEADS),
                               seg(_IN_AA, _IN_ZB), zeros(8 - GDN_HEADS),
                               seg(_IN_DT, _IN_GLU)], axis=-1)
    return main, small, jnp.swapaxes(small_t, 1, 2).astype(BF16)


def _lane_pad(v, rows=8):
    out = jnp.zeros((v[0].shape[0], rows, LANES), F32)
    for i, a in enumerate(v):
        out = out.at[:, i, :a.shape[1]].set(a)
    return out


def _row_rep(v, rows):
    depth, n = v.shape
    out = jnp.zeros((depth, rows, LANES), F32)
    return out.at[:, :n, :].set(jnp.broadcast_to(v[:, :, None], (depth, n, LANES)))


def kernel(x_prompt, x_sample, state_gdn, state_gdn_conv, state_ssm, state_ssm_conv, state_cfm_conv,
           norm_w, w_in, gdn_conv_w, gdn_a_log, gdn_dt_bias, gdn_norm_w, gdn_w_o,
           ssm_conv_w, ssm_conv_b, ssm_a_log, ssm_dt_bias, ssm_d, ssm_norm_w, ssm_w_o,
           cfm_conv_w, cfm_conv_b, cfm_ln_w, cfm_ln_b, cfm_w_o, w_out, final_norm_w):
    depth = w_in.shape[0]
    batch, seq, _ = x_prompt.shape
    n_tok = x_sample.shape[0]

    w_main, w_small, w_small_t = _prep_w_in(w_in)
    wa, wb, wc, wo = (w.astype(BF16) for w in (gdn_w_o, ssm_w_o, cfm_w_o, w_out))
    gdn_pc = _lane_pad([gdn_a_log, gdn_dt_bias])
    gdn_pr = jnp.concatenate([_row_rep(gdn_a_log, 8), _row_rep(gdn_dt_bias, 8)], axis=1)
    ssm_pc = _lane_pad([ssm_a_log, ssm_dt_bias])
    ssm_pr = jnp.concatenate([_row_rep(ssm_a_log, 16), _row_rep(ssm_dt_bias, 16)], axis=1)
    ssm_d_row = jnp.repeat(ssm_d, SSM_P, axis=1)[:, None, :]
    final_w = final_norm_w[None, :]

    hp = x_prompt.reshape(batch * seq, D_MODEL)
    hs = x_sample.reshape(n_tok, D_MODEL)
    p_states, s_states = [], []
    for i in range(depth):
        final = i == depth - 1
        row = lambda a: a[i][None, :]

        proj, sm, smt = _inproj(hp, row(norm_w), w_main[i], w_small[i], w_small_t[i], tm=1024)
        oa, gdn_s, gdn_cv = _gdn_prompt(proj, sm, smt, gdn_conv_w[i], gdn_pc[i], gdn_pr[i],
                                        row(gdn_norm_w), batch, seq, t_len=256)
        yb, ssm_s, ssm_cv = _ssd_prompt(proj, sm, smt, ssm_conv_w[i], row(ssm_conv_b), ssm_pc[i], ssm_pr[i],
                                        ssm_d_row[i], row(ssm_norm_w), batch, seq, t_len=256)
        uc, cfm_cv = _cfm_prompt(proj, cfm_conv_w[i], row(cfm_conv_b), row(cfm_ln_w), row(cfm_ln_b),
                                 batch, seq, t_len=256)
        hp = _out_proj(hp, oa, yb, uc, proj, wa[i], wb[i], wc[i], wo[i], final_w, tm=256, final=final)
        p_states.append((gdn_s, gdn_cv[:, CONV_HIST - 3:], ssm_s, ssm_cv[:, CONV_HIST - 3:],
                         cfm_cv[:, CFM_HIST - (CFM_K - 1):]))

        proj, sm, _ = _inproj(hs, row(norm_w), w_main[i], w_small[i], w_small_t[i], tm=n_tok)
        oa, gdn_s, gdn_buf = _gdn_sample(proj, sm, jnp.swapaxes(state_gdn_conv[i], 0, 1), gdn_conv_w[i],
                                         gdn_pc[i], row(gdn_norm_w), state_gdn[i])
        yb, ssm_s, ssm_buf = _ssd_sample(proj, sm, jnp.swapaxes(state_ssm_conv[i], 0, 1), ssm_conv_w[i],
                                         row(ssm_conv_b), ssm_pc[i], ssm_d_row[i], row(ssm_norm_w),
                                         state_ssm[i])
        uc, cfm_buf = _cfm_sample(proj, jnp.swapaxes(state_cfm_conv[i], 0, 1), cfm_conv_w[i],
                                  row(cfm_conv_b), row(cfm_ln_w), row(cfm_ln_b))
        hs = _out_proj(hs, oa, yb, uc, proj, wa[i], wb[i], wc[i], wo[i], final_w, tm=n_tok, final=final)
        s_states.append((gdn_s, jnp.swapaxes(gdn_buf, 0, 1), ssm_s, jnp.swapaxes(ssm_buf, 0, 1),
                         jnp.swapaxes(cfm_buf, 0, 1)))

    stack = lambda states, j: jnp.stack([s[j] for s in states])
    return (hp.reshape(batch, seq, D_MODEL), hs.reshape(n_tok, 1, D_MODEL),
            stack(p_states, 0), stack(p_states, 1), stack(p_states, 2), stack(p_states, 3), stack(p_states, 4),
            stack(s_states, 0), stack(s_states, 1), stack(s_states, 2), stack(s_states, 3), stack(s_states, 4))
```

```python
import functools

import jax
import jax.numpy as jnp
from jax import lax
from jax.experimental import pallas as pl
from jax.experimental.pallas import tpu as pltpu

F32 = jnp.float32
BF16 = jnp.bfloat16
HIGHEST = lax.Precision.HIGHEST

D_MODEL = 1024
GDN_HEADS = 4
GDN_D = 128
GDN_QK_W = GDN_HEADS * GDN_D
GDN_CH = 3 * GDN_QK_W
SSM_W = 1024
SSM_P = 64
SSM_HEADS = 16
SSM_GROUPS = 2
SSM_RPG = SSM_HEADS // SSM_GROUPS
SSM_GW = SSM_RPG * SSM_P
SSM_N = 128
SSM_XBC = SSM_W + 2 * SSM_GROUPS * SSM_N
CFM_W = 512
CFM_K = 31
CFM_HIST = 32
CONV_HIST = 8
CHUNK = 64
EPS = 1e-6
NEG = -1e30
LANES = 128
SMALL_W = 3 * LANES
SMALL_T_ROWS = 32

MAIN_W = 9216
COL_QKV, COL_XBC, COL_ZB, COL_GLU, COL_ZA, COL_GC, COL_GATES = 0, 1536, 3072, 4096, 5120, 5632, 6144

VMEM_LIMIT = 56 * 1024 * 1024


def _sds(shape, dtype):
    return jax.ShapeDtypeStruct(shape, dtype)


def _params(n_axes):
    return pltpu.CompilerParams(dimension_semantics=("arbitrary",) * n_axes,
                                vmem_limit_bytes=VMEM_LIMIT)


def _sigmoid(x):
    return jax.nn.sigmoid(x)


def _silu(x):
    return x * jax.nn.sigmoid(x)


def _softplus(x):
    return jnp.maximum(x, 0.0) + jnp.log1p(jnp.exp(-jnp.abs(x)))


def _mm(a, b):
    return jnp.dot(a.astype(BF16), b.astype(BF16), preferred_element_type=F32)


def _mm_nt(a, b):
    return lax.dot_general(a.astype(BF16), b.astype(BF16), (((1,), (1,)), ((), ())),
                           preferred_element_type=F32)


def _mm_tn(a, b):
    return lax.dot_general(a.astype(BF16), b.astype(BF16), (((0,), (0,)), ((), ())),
                           preferred_element_type=F32)


def _mm_exact(a, b):
    return jnp.dot(a, b, precision=HIGHEST, preferred_element_type=F32)


def _rms(x, w):
    return x * lax.rsqrt(jnp.mean(x * x, axis=-1, keepdims=True) + EPS) * w


def _tri_masks():
    r = lax.broadcasted_iota(jnp.int32, (CHUNK, CHUNK), 0)
    c = lax.broadcasted_iota(jnp.int32, (CHUNK, CHUNK), 1)
    return r >= c, r > c, r <= c


def _expand_heads(x, emat):
    hi = x.astype(BF16)
    r1 = x - hi.astype(F32)
    mid = r1.astype(BF16)
    lo = (r1 - mid.astype(F32)).astype(BF16)
    dot = functools.partial(jnp.dot, preferred_element_type=F32)
    return dot(hi, emat) + dot(mid, emat) + dot(lo, emat)


def _expand_matrix():
    r = lax.broadcasted_iota(jnp.int32, (LANES, SSM_W), 0)
    c = lax.broadcasted_iota(jnp.int32, (LANES, SSM_W), 1)
    return jnp.where((c >> 6) == r, 1.0, 0.0).astype(BF16)


def _causal_conv(x, hist, w_ref, taps):
    t_len = x.shape[0]
    h_len = hist.shape[0]
    xe = jnp.concatenate([hist, x], axis=0)
    acc = x * w_ref[taps - 1:taps, :]
    rolled = {}
    for s in range(1, taps):
        a, b = divmod(s, 8)
        if b == 0:
            src = xe
        else:
            if b not in rolled:
                rolled[b] = pltpu.roll(xe, b, axis=0)
            src = rolled[b]
        lo = h_len - 8 * a
        acc = acc + src[lo:lo + t_len, :] * w_ref[taps - 1 - s:taps - s, :]
    return acc, xe[t_len:, :]


def _inproj_kernel(x_ref, nw_ref, w_ref, ws_ref, wst_ref, o_ref, os_ref, ost_ref, h_ref):
    @pl.when(pl.program_id(1) == 0)
    def _():
        h = _rms(x_ref[...], nw_ref[...]).astype(BF16)
        h_ref[...] = h
        os_ref[...] = jnp.dot(h, ws_ref[...], preferred_element_type=F32)
        ost_ref[...] = lax.dot_general(wst_ref[...], h, (((1,), (1,)), ((), ())),
                                       preferred_element_type=F32)

    o_ref[...] = jnp.dot(h_ref[...], w_ref[...], preferred_element_type=F32)


def _inproj(x, norm_w, w_main, w_small, w_small_t, layer):
    m = x.shape[0]
    tm = min(m, 1024)
    tn = 1024
    return pl.pallas_call(
        _inproj_kernel,
        grid=(m // tm, MAIN_W // tn),
        in_specs=[
            pl.BlockSpec((tm, D_MODEL), lambda i, j: (i, 0)),
            pl.BlockSpec((1, D_MODEL), lambda i, j: (0, 0)),
            pl.BlockSpec((None, D_MODEL, tn), lambda i, j: (layer, 0, j)),
            pl.BlockSpec((D_MODEL, SMALL_W), lambda i, j: (0, 0)),
            pl.BlockSpec((SMALL_T_ROWS, D_MODEL), lambda i, j: (0, 0)),
        ],
        out_specs=[
            pl.BlockSpec((tm, tn), lambda i, j: (i, j)),
            pl.BlockSpec((tm, SMALL_W), lambda i, j: (i, 0)),
            pl.BlockSpec((SMALL_T_ROWS, tm), lambda i, j: (0, i)),
        ],
        out_shape=[_sds((m, MAIN_W), F32), _sds((m, SMALL_W), F32), _sds((SMALL_T_ROWS, m), F32)],
        scratch_shapes=[pltpu.VMEM((tm, D_MODEL), BF16)],
        compiler_params=_params(2),
        name="inproj",
    )(x, norm_w, w_main, w_small, w_small_t)


GDN_GROUP = 4
GDN_T = GDN_GROUP * CHUNK
GDN_CAT = GDN_GROUP * CHUNK
CHUNK_SHIFT = 6
GDN_D_SHIFT = 7


def _chunk_cumsum(x):
    pos = lax.broadcasted_iota(jnp.int32, x.shape, 0) & (CHUNK - 1)
    d = 1
    while d < CHUNK:
        x = x + jnp.where(pos >= d, pltpu.roll(x, d, axis=0), 0.0)
        d *= 2
    return x


def _lane_cat(x):
    return jnp.concatenate([x[c * CHUNK:(c + 1) * CHUNK, :] for c in range(GDN_GROUP)], axis=1)


def _gdn_prompt_kernel(qkv_ref, za_ref, sm_ref, cw_ref, pc_ref, nw_ref,
                       o_ref, st_ref, cv_ref, s_scr, hist_scr, o_scr):
    @pl.when(pl.program_id(1) == 0)
    def _():
        s_scr[...] = jnp.zeros_like(s_scr)
        hist_scr[...] = jnp.zeros_like(hist_scr)

    x = qkv_ref[...]
    y, new_hist = _causal_conv(x, hist_scr[...], cw_ref, 4)
    hist_scr[...] = new_hist
    cv_ref[0] = new_hist
    qkv = _silu(y)

    heads = range(GDN_HEADS)
    q_n, k_n, v_n = [], [], []
    for h in heads:
        q = qkv[:, h * GDN_D:(h + 1) * GDN_D]
        k = qkv[:, GDN_QK_W + h * GDN_D:GDN_QK_W + (h + 1) * GDN_D]
        q_n.append(q * lax.rsqrt(jnp.sum(q * q, axis=-1, keepdims=True) + EPS) * (GDN_D ** -0.5))
        k_n.append(k * lax.rsqrt(jnp.sum(k * k, axis=-1, keepdims=True) + EPS))
        v_n.append(qkv[:, 2 * GDN_QK_W + h * GDN_D:2 * GDN_QK_W + (h + 1) * GDN_D])

    sm = sm_ref[...]
    beta_c = _sigmoid(sm[:, 0:LANES])
    g_c = -jnp.exp(pc_ref[0:1, :]) * _softplus(sm[:, LANES:2 * LANES] + pc_ref[1:2, :])
    gam_c = _chunk_cumsum(g_c)
    glast_c = jnp.concatenate(
        [jnp.broadcast_to(gam_c[(c + 1) * CHUNK - 1:(c + 1) * CHUNK, :], (CHUNK, LANES))
         for c in range(GDN_GROUP)], axis=0)
    eg_c = jnp.exp(gam_c)
    ekd_c = jnp.exp(glast_c - gam_c)
    dtot_c = jnp.exp(glast_c)
    bg_c = beta_c * eg_c
    gam_r = gam_c.T
    col = lambda a, h: a[:, h:h + 1]

    iota = lambda shape, d: lax.broadcasted_iota(jnp.int32, shape, d)
    t_i = iota((CHUNK, GDN_CAT), 0)
    s_i = iota((CHUNK, GDN_CAT), 1) & (CHUNK - 1)
    incl, strict = t_i >= s_i, t_i > s_i
    eye_cat = jnp.where(t_i == s_i, 1.0, 0.0)
    bd_mask = jnp.where((iota((GDN_CAT, GDN_CAT), 0) >> CHUNK_SHIFT) == (iota((GDN_CAT, GDN_CAT), 1) >> CHUNK_SHIFT),
                        1.0, 0.0).astype(BF16)
    k_mask = jnp.where((iota((GDN_T, GDN_GROUP * GDN_D), 0) >> CHUNK_SHIFT)
                       == (iota((GDN_T, GDN_GROUP * GDN_D), 1) >> GDN_D_SHIFT), 1.0, 0.0).astype(BF16)
    low_half = iota((1, LANES), 1) < CHUNK

    def block_diag(y_cat):
        return jnp.concatenate([y_cat.astype(BF16)] * GDN_GROUP, axis=0) * bd_mask

    kb = [k_n[h] * col(beta_c, h) for h in heads]
    aq = []
    for h in heads:
        rhs_nt = jnp.concatenate([k_n[h].astype(BF16)] * GDN_GROUP, axis=1) * k_mask
        aq.append(_mm_nt(jnp.concatenate([_lane_cat(kb[h]), _lane_cat(q_n[h])], axis=0), rhs_nt))

    dincl = []
    for h in heads:
        gcol = col(gam_c, h)
        tiles = []
        for m in range(GDN_GROUP // 2):
            lo = jnp.broadcast_to(gcol[2 * m * CHUNK:(2 * m + 1) * CHUNK, :], (CHUNK, LANES))
            hi = jnp.broadcast_to(gcol[(2 * m + 1) * CHUNK:(2 * m + 2) * CHUNK, :], (CHUNK, LANES))
            tiles.append(jnp.where(low_half, lo, hi))
        dincl.append(jnp.exp(jnp.where(incl, jnp.concatenate(tiles, axis=1) - gam_r[h:h + 1, :], NEG)))
    n0 = [aq[h][:CHUNK, :] * jnp.where(strict, dincl[h], 0.0) for h in heads]
    attn_bd = [block_diag(aq[h][CHUNK:, :] * dincl[h]) for h in heads]

    p = [eye_cat - n0[h] for h in heads]
    pw = [_mm(n0[h], block_diag(n0[h])) for h in heads]
    for _ in range(4):
        both = [_mm(jnp.concatenate([pw[h], p[h]], axis=0), block_diag(pw[h])) for h in heads]
        pw = [both[h][:CHUNK, :] for h in heads]
        p = [p[h] + both[h][CHUNK:, :] for h in heads]
    tinv = [p[h] + _mm(p[h], block_diag(pw[h])) for h in heads]

    wu = [_mm(block_diag(tinv[h]),
              jnp.concatenate([k_n[h] * col(bg_c, h), v_n[h] * col(beta_c, h)], axis=1)) for h in heads]
    qg = [q_n[h] * col(eg_c, h) for h in heads]
    kd = [k_n[h] * col(ekd_c, h) for h in heads]

    state = [s_scr[h] for h in heads]
    zeros = jnp.zeros((CHUNK, GDN_D), F32)
    for c in range(GDN_GROUP):
        rows = slice(c * CHUNK, (c + 1) * CHUNK)
        ws_qs = [_mm(jnp.concatenate([wu[h][rows, :GDN_D], qg[h][rows, :]], axis=0), state[h]) for h in heads]
        u = [wu[h][rows, GDN_D:] - ws_qs[h][:CHUNK, :] for h in heads]
        for h in heads:
            u_pad = jnp.concatenate([zeros] * c + [u[h]] + [zeros] * (GDN_GROUP - 1 - c), axis=0)
            o_scr[rows, h * GDN_D:(h + 1) * GDN_D] = ws_qs[h][CHUNK:, :] + _mm(attn_bd[h][rows, :], u_pad)
        state = [state[h] * dtot_c[c * CHUNK:c * CHUNK + 1, h:h + 1] + _mm_tn(kd[h][rows, :], u[h])
                 for h in heads]
    for h in heads:
        s_scr[h] = state[h]

    o = o_scr[...]
    za = za_ref[...]
    outs = []
    for h in range(GDN_HEADS):
        sl = slice(h * GDN_D, (h + 1) * GDN_D)
        outs.append(_rms(o[:, sl], nw_ref[...]) * _silu(za[:, sl]))
    o_ref[...] = jnp.concatenate(outs, axis=1).astype(BF16)
    st_ref[0] = s_scr[...]


def _gdn_prompt(proj, sm, conv_w, pc, norm_w, batch, seq):
    t_len = GDN_T
    nt = seq // t_len
    row = lambda b, t: b * nt + t
    const = lambda shape: pl.BlockSpec(shape, lambda b, t: (0,) * len(shape))
    return pl.pallas_call(
        _gdn_prompt_kernel,
        grid=(batch, nt),
        in_specs=[
            pl.BlockSpec((t_len, GDN_CH), lambda b, t: (row(b, t), COL_QKV // GDN_CH)),
            pl.BlockSpec((t_len, GDN_QK_W), lambda b, t: (row(b, t), COL_ZA // GDN_QK_W)),
            pl.BlockSpec((t_len, SMALL_W), lambda b, t: (row(b, t), 0)),
            const((4, GDN_CH)), const((8, LANES)), const((1, GDN_D)),
        ],
        out_specs=[
            pl.BlockSpec((t_len, GDN_QK_W), lambda b, t: (row(b, t), 0)),
            pl.BlockSpec((1, GDN_HEADS, GDN_D, GDN_D), lambda b, t: (b, 0, 0, 0)),
            pl.BlockSpec((1, CONV_HIST, GDN_CH), lambda b, t: (b, 0, 0)),
        ],
        out_shape=[_sds((batch * seq, GDN_QK_W), BF16),
                   _sds((batch, GDN_HEADS, GDN_D, GDN_D), F32),
                   _sds((batch, CONV_HIST, GDN_CH), F32)],
        scratch_shapes=[pltpu.VMEM((GDN_HEADS, GDN_D, GDN_D), F32),
                        pltpu.VMEM((CONV_HIST, GDN_CH), F32),
                        pltpu.VMEM((t_len, GDN_QK_W), F32)],
        compiler_params=_params(2),
        name="gdn_prompt",
    )(proj, proj, sm, conv_w, pc, norm_w)


def _ssd_prompt_kernel(xbc_ref, zb_ref, sm_ref, smt_ref, cw_ref, cb_ref, pc_ref, pr_ref, d_ref, nw_ref,
                       o_ref, st_ref, cv_ref, st_scr, hist_scr, y_scr, *, t_len):
    t_idx = pl.program_id(1)

    @pl.when(t_idx == 0)
    def _():
        st_scr[...] = jnp.zeros_like(st_scr)
        hist_scr[...] = jnp.zeros_like(hist_scr)

    x = xbc_ref[...]
    y, new_hist = _causal_conv(x, hist_scr[...], cw_ref, 4)
    hist_scr[...] = new_hist
    cv_ref[0] = new_hist
    xbc = _silu(y + cb_ref[...])
    xs = xbc[:, :SSM_W]
    bm = xbc[:, SSM_W:SSM_W + SSM_GROUPS * SSM_N]
    cm = xbc[:, SSM_W + SSM_GROUPS * SSM_N:]

    lane = lax.broadcasted_iota(jnp.int32, (1, LANES), 1)
    dt_c = jnp.where(lane < SSM_HEADS,
                     _softplus(sm_ref[:, 2 * LANES:3 * LANES] + pc_ref[1:2, :]), 0.0)
    ga_c = dt_c * (-jnp.exp(pc_ref[0:1, :]))
    nrep = t_len // LANES
    dt_r = _softplus(smt_ref[16:32, :] + jnp.tile(pr_ref[16:32, :], (1, nrep)))
    ga_r = dt_r * (-jnp.exp(jnp.tile(pr_ref[0:16, :], (1, nrep))))

    incl, _, upper = _tri_masks()
    ltri = jnp.where(incl, 1.0, 0.0).astype(F32)
    utri = jnp.where(upper, 1.0, 0.0).astype(F32)
    emat = _expand_matrix()
    low_half = lane < SSM_P

    for c in range(t_len // CHUNK):
        r0 = c * CHUNK
        gam_c = _mm_exact(ltri, ga_c[r0:r0 + CHUNK, :])
        gam_r = _mm_exact(ga_r[:, r0:r0 + CHUNK], utri)
        dtr = dt_r[:, r0:r0 + CHUNK]
        glast = gam_c[CHUNK - 1:CHUNK, :]
        e1 = _expand_heads(jnp.exp(gam_c), emat)
        e2 = _expand_heads(jnp.exp(glast - gam_c) * dt_c[r0:r0 + CHUNK, :], emat)
        dtot = e1[CHUNK - 1:CHUNK, :]
        xs_c = xs[r0:r0 + CHUNK, :]
        bm_c = bm[r0:r0 + CHUNK, :]
        cm_c = cm[r0:r0 + CHUNK, :]
        cbs = [_mm_nt(cm_c[:, g * SSM_N:(g + 1) * SSM_N], bm_c[:, g * SSM_N:(g + 1) * SSM_N])
               for g in range(SSM_GROUPS)]
        pairs = []
        for j in range(SSM_HEADS // 2):
            xp = xs_c[:, j * LANES:(j + 1) * LANES]
            acc = None
            for half in range(2):
                h = 2 * j + half
                m = (cbs[h // SSM_RPG]
                     * jnp.exp(jnp.where(incl, gam_c[:, h:h + 1] - gam_r[h:h + 1, :], NEG))
                     * dtr[h:h + 1, :])
                xm = jnp.where(low_half if half == 0 else jnp.logical_not(low_half), xp, 0.0)
                part = _mm(m, xm)
                acc = part if acc is None else acc + part
            pairs.append(acc)
        y_intra = jnp.concatenate(pairs, axis=1)
        y_inter = jnp.concatenate(
            [_mm(cm_c[:, g * SSM_N:(g + 1) * SSM_N], st_scr[g]) for g in range(SSM_GROUPS)], axis=1)
        y_scr[r0:r0 + CHUNK, :] = y_intra + y_inter * e1 + d_ref[...] * xs_c
        xe = xs_c * e2
        for g in range(SSM_GROUPS):
            sl = slice(g * SSM_GW, (g + 1) * SSM_GW)
            st_scr[g] = st_scr[g] * dtot[:, sl] + _mm_tn(bm_c[:, g * SSM_N:(g + 1) * SSM_N], xe[:, sl])

    o_ref[...] = _rms(y_scr[...] * _silu(zb_ref[...]), nw_ref[...]).astype(BF16)

    @pl.when(t_idx == pl.num_programs(1) - 1)
    def _():
        for g in range(SSM_GROUPS):
            st_ref[0, g * SSM_RPG:(g + 1) * SSM_RPG] = st_scr[g].T.reshape(SSM_RPG, SSM_P, SSM_N)


def _ssd_prompt(proj, sm, smt, conv_w, conv_b, pc, pr, d_row, norm_w, batch, seq, t_len):
    nt = seq // t_len
    row = lambda b, t: b * nt + t
    kern = functools.partial(_ssd_prompt_kernel, t_len=t_len)
    const = lambda shape: pl.BlockSpec(shape, lambda b, t: (0,) * len(shape))
    return pl.pallas_call(
        kern,
        grid=(batch, nt),
        in_specs=[
            pl.BlockSpec((t_len, SSM_XBC), lambda b, t: (row(b, t), COL_XBC // SSM_XBC)),
            pl.BlockSpec((t_len, SSM_W), lambda b, t: (row(b, t), COL_ZB // SSM_W)),
            pl.BlockSpec((t_len, SMALL_W), lambda b, t: (row(b, t), 0)),
            pl.BlockSpec((SMALL_T_ROWS, t_len), lambda b, t: (0, row(b, t))),
            const((4, SSM_XBC)), const((1, SSM_XBC)), const((8, LANES)), const((32, LANES)),
            const((1, SSM_W)), const((1, SSM_W)),
        ],
        out_specs=[
            pl.BlockSpec((t_len, SSM_W), lambda b, t: (row(b, t), 0)),
            pl.BlockSpec((1, SSM_HEADS, SSM_P, SSM_N), lambda b, t: (b, 0, 0, 0)),
            pl.BlockSpec((1, CONV_HIST, SSM_XBC), lambda b, t: (b, 0, 0)),
        ],
        out_shape=[_sds((batch * seq, SSM_W), BF16),
                   _sds((batch, SSM_HEADS, SSM_P, SSM_N), F32),
                   _sds((batch, CONV_HIST, SSM_XBC), F32)],
        scratch_shapes=[pltpu.VMEM((SSM_GROUPS, SSM_N, SSM_GW), F32),
                        pltpu.VMEM((CONV_HIST, SSM_XBC), F32),
                        pltpu.VMEM((t_len, SSM_W), F32)],
        compiler_params=_params(2),
        name="ssd_prompt",
    )(proj, proj, sm, smt, conv_w, conv_b, pc, pr, d_row, norm_w)


def _layernorm_silu_gate(u, gate, lnw, lnb):
    mu = jnp.mean(u, axis=-1, keepdims=True)
    uc = u - mu
    var = jnp.mean(uc * uc, axis=-1, keepdims=True)
    return _silu(uc * lax.rsqrt(var + EPS) * lnw + lnb) * _silu(gate)


def _cfm_prompt_kernel(glu_ref, gc_ref, cw_ref, cb_ref, lnw_ref, lnb_ref, o_ref, cv_ref, hist_scr):
    @pl.when(pl.program_id(1) == 0)
    def _():
        hist_scr[...] = jnp.zeros_like(hist_scr)

    glu = glu_ref[...]
    u0 = glu[:, :CFM_W] * _sigmoid(glu[:, CFM_W:])
    u, new_hist = _causal_conv(u0, hist_scr[...], cw_ref, CFM_K)
    hist_scr[...] = new_hist
    cv_ref[0] = new_hist
    o_ref[...] = _layernorm_silu_gate(u + cb_ref[...], gc_ref[...], lnw_ref[...], lnb_ref[...]).astype(BF16)


def _cfm_prompt(proj, conv_w, conv_b, ln_w, ln_b, batch, seq, t_len):
    nt = seq // t_len
    row = lambda b, t: b * nt + t
    const = lambda shape: pl.BlockSpec(shape, lambda b, t: (0,) * len(shape))
    return pl.pallas_call(
        _cfm_prompt_kernel,
        grid=(batch, nt),
        in_specs=[
            pl.BlockSpec((t_len, 2 * CFM_W), lambda b, t: (row(b, t), COL_GLU // (2 * CFM_W))),
            pl.BlockSpec((t_len, CFM_W), lambda b, t: (row(b, t), COL_GC // CFM_W)),
            const((CFM_K, CFM_W)), const((1, CFM_W)), const((1, CFM_W)), const((1, CFM_W)),
        ],
        out_specs=[
            pl.BlockSpec((t_len, CFM_W), lambda b, t: (row(b, t), 0)),
            pl.BlockSpec((1, CFM_HIST, CFM_W), lambda b, t: (b, 0, 0)),
        ],
        out_shape=[_sds((batch * seq, CFM_W), BF16), _sds((batch, CFM_HIST, CFM_W), F32)],
        scratch_shapes=[pltpu.VMEM((CFM_HIST, CFM_W), F32)],
        compiler_params=_params(2),
        name="cfm_prompt",
    )(proj, proj, conv_w, conv_b, ln_w, ln_b)


def _out_kernel(x_ref, oa_ref, yb_ref, uc_ref, g_ref, wa_ref, wb_ref, wc_ref, wo_ref, fw_ref, o_ref, *, final):
    dot = functools.partial(jnp.dot, preferred_element_type=F32)
    g = g_ref[...]
    merged = (_sigmoid(g[:, :D_MODEL]) * dot(oa_ref[...], wa_ref[...])
              + _sigmoid(g[:, D_MODEL:2 * D_MODEL]) * dot(yb_ref[...], wb_ref[...])
              + _sigmoid(g[:, 2 * D_MODEL:]) * dot(uc_ref[...], wc_ref[...]))
    out = x_ref[...] + dot(merged.astype(BF16), wo_ref[...])
    if final:
        out = _rms(out, fw_ref[...])
    o_ref[...] = out


def _out_proj(x, oa, yb, uc, proj, wa, wb, wc, wo, final_w, tm, final):
    m = x.shape[0]
    const = lambda shape: pl.BlockSpec(shape, lambda i: (0,) * len(shape))
    return pl.pallas_call(
        functools.partial(_out_kernel, final=final),
        grid=(m // tm,),
        in_specs=[
            pl.BlockSpec((tm, D_MODEL), lambda i: (i, 0)),
            pl.BlockSpec((tm, GDN_QK_W), lambda i: (i, 0)),
            pl.BlockSpec((tm, SSM_W), lambda i: (i, 0)),
            pl.BlockSpec((tm, CFM_W), lambda i: (i, 0)),
            pl.BlockSpec((tm, 3 * D_MODEL), lambda i: (i, COL_GATES // (3 * D_MODEL))),
            const((GDN_QK_W, D_MODEL)), const((SSM_W, D_MODEL)), const((CFM_W, D_MODEL)),
            const((D_MODEL, D_MODEL)), const((1, D_MODEL)),
        ],
        out_specs=pl.BlockSpec((tm, D_MODEL), lambda i: (i, 0)),
        out_shape=_sds((m, D_MODEL), F32),
        compiler_params=_params(1),
        name="merge_out",
    )(x, oa, yb, uc, proj, wa, wb, wc, wo, final_w)


TOK_GROUP = 8


def _gdn_sample_kernel(qkv_ref, za_ref, sm_ref, buf_ref, cw_ref, pc_ref, nw_ref, s_ref,
                       o_ref, so_ref, bo_ref,
                       kt_scr, qt_scr, v_scr, a_scr, b_scr, qk_scr, o_scr):
    tg = pl.program_id(0)

    @pl.when(tg == 0)
    def _():
        x = qkv_ref[...]
        y = (cw_ref[0:1, :] * buf_ref[0] + cw_ref[1:2, :] * buf_ref[1]
             + cw_ref[2:3, :] * buf_ref[2] + cw_ref[3:4, :] * x)
        bo_ref[0] = buf_ref[1]
        bo_ref[1] = buf_ref[2]
        bo_ref[2] = x
        qkv = _silu(y)
        sm = sm_ref[...]
        beta = _sigmoid(sm[:, 0:LANES])
        decay = jnp.exp(-jnp.exp(pc_ref[0:1, :]) * _softplus(sm[:, LANES:2 * LANES] + pc_ref[1:2, :]))
        n_tok = x.shape[0]
        for h in range(GDN_HEADS):
            q = qkv[:, h * GDN_D:(h + 1) * GDN_D]
            k = qkv[:, GDN_QK_W + h * GDN_D:GDN_QK_W + (h + 1) * GDN_D]
            q = q * lax.rsqrt(jnp.sum(q * q, axis=-1, keepdims=True) + EPS) * (GDN_D ** -0.5)
            k = k * lax.rsqrt(jnp.sum(k * k, axis=-1, keepdims=True) + EPS)
            kt_scr[h] = k.T
            qt_scr[h] = q.T
            a_scr[h] = jnp.broadcast_to(decay[:, h:h + 1], (n_tok, LANES))
            b_scr[h] = jnp.broadcast_to(beta[:, h:h + 1], (n_tok, LANES))
            qk_scr[h] = jnp.broadcast_to(jnp.sum(q * k, axis=-1, keepdims=True), (n_tok, LANES))
        v_scr[...] = qkv[:, 2 * GDN_QK_W:]

    shift = (LANES - TOK_GROUP * tg) & (LANES - 1)
    rows = pl.ds(pl.multiple_of(tg * TOK_GROUP, TOK_GROUP), TOK_GROUP)
    for h in range(GDN_HEADS):
        ktg = pltpu.roll(kt_scr[h], shift, axis=1)
        qtg = pltpu.roll(qt_scr[h], shift, axis=1)
        a_blk = a_scr[h, rows, :]
        b_blk = b_scr[h, rows, :]
        qk_blk = qk_scr[h, rows, :]
        v_blk = v_scr[rows, h * GDN_D:(h + 1) * GDN_D]
        o_rows = []
        for j in range(TOK_GROUP):
            s = s_ref[j, h]
            kcol = ktg[:, j:j + 1]
            qcol = qtg[:, j:j + 1]
            ks = jnp.sum(s * kcol, axis=0, keepdims=True)
            qs = jnp.sum(s * qcol, axis=0, keepdims=True)
            a_row = a_blk[j:j + 1, :]
            delta = b_blk[j:j + 1, :] * (v_blk[j:j + 1, :] - a_row * ks)
            so_ref[j, h] = a_row * s + kcol * delta
            o_rows.append(a_row * qs + qk_blk[j:j + 1, :] * delta)
        o_scr[rows, h * GDN_D:(h + 1) * GDN_D] = jnp.concatenate(o_rows, axis=0)

    @pl.when(tg == pl.num_programs(0) - 1)
    def _():
        o = o_scr[...]
        za = za_ref[...]
        outs = []
        for h in range(GDN_HEADS):
            sl = slice(h * GDN_D, (h + 1) * GDN_D)
            outs.append(_rms(o[:, sl], nw_ref[...]) * _silu(za[:, sl]))
        o_ref[...] = jnp.concatenate(outs, axis=1).astype(BF16)


def _in_place_state(kernel_fn, inputs, in_specs, prev_out, out_index):
    if prev_out is None:
        return kernel_fn, inputs, in_specs, {}
    n = len(inputs)
    wrapped = lambda *refs: kernel_fn(*refs[:n], *refs[n + 1:])
    return wrapped, inputs + [prev_out], in_specs + [pl.BlockSpec(memory_space=pl.ANY)], {n: out_index}


def _gdn_sample(proj, sm, buf_t, conv_w, pc, norm_w, state_all, layer, prev_out):
    n_tok = proj.shape[0]
    const = lambda shape: pl.BlockSpec(shape, lambda g: (0,) * len(shape))
    st_spec = pl.BlockSpec((None, TOK_GROUP, GDN_HEADS, GDN_D, GDN_D), lambda g: (layer, g, 0, 0, 0))
    in_specs = [
        pl.BlockSpec((n_tok, GDN_CH), lambda g: (0, COL_QKV // GDN_CH)),
        pl.BlockSpec((n_tok, GDN_QK_W), lambda g: (0, COL_ZA // GDN_QK_W)),
        const((n_tok, SMALL_W)), const((3, n_tok, GDN_CH)), const((4, GDN_CH)),
        const((8, LANES)), const((1, GDN_D)), st_spec,
    ]
    kern, inputs, in_specs, aliases = _in_place_state(
        _gdn_sample_kernel, [proj, proj, sm, buf_t, conv_w, pc, norm_w, state_all], in_specs, prev_out, 1)
    return pl.pallas_call(
        kern,
        grid=(n_tok // TOK_GROUP,),
        in_specs=in_specs,
        out_specs=[const((n_tok, GDN_QK_W)), st_spec, const((3, n_tok, GDN_CH))],
        out_shape=[_sds((n_tok, GDN_QK_W), BF16), _sds(state_all.shape, F32), _sds((3, n_tok, GDN_CH), F32)],
        input_output_aliases=aliases,
        scratch_shapes=[pltpu.VMEM((GDN_HEADS, GDN_D, n_tok), F32),
                        pltpu.VMEM((GDN_HEADS, GDN_D, n_tok), F32),
                        pltpu.VMEM((n_tok, GDN_QK_W), F32),
                        pltpu.VMEM((GDN_HEADS, n_tok, LANES), F32),
                        pltpu.VMEM((GDN_HEADS, n_tok, LANES), F32),
                        pltpu.VMEM((GDN_HEADS, n_tok, LANES), F32),
                        pltpu.VMEM((n_tok, GDN_QK_W), F32)],
        compiler_params=_params(1),
        name="gdn_sample",
    )(*inputs)


def _ssd_sample_kernel(xbc_ref, zb_ref, sm_ref, buf_ref, cw_ref, cb_ref, pc_ref, d_ref, nw_ref, s_ref,
                       o_ref, so_ref, bo_ref,
                       xt_scr, xs_scr, bm_scr, cm_scr, a_scr, yt_scr):
    tg = pl.program_id(0)
    n_tok = xbc_ref.shape[0]

    @pl.when(tg == 0)
    def _():
        x = xbc_ref[...]
        y = (cw_ref[0:1, :] * buf_ref[0] + cw_ref[1:2, :] * buf_ref[1]
             + cw_ref[2:3, :] * buf_ref[2] + cw_ref[3:4, :] * x)
        bo_ref[0] = buf_ref[1]
        bo_ref[1] = buf_ref[2]
        bo_ref[2] = x
        xbc = _silu(y + cb_ref[...])
        xs = xbc[:, :SSM_W]
        xs_scr[...] = xs
        bm_scr[...] = xbc[:, SSM_W:SSM_W + SSM_GROUPS * SSM_N]
        cm_scr[...] = xbc[:, SSM_W + SSM_GROUPS * SSM_N:]
        lane = lax.broadcasted_iota(jnp.int32, (1, LANES), 1)
        dt = jnp.where(lane < SSM_HEADS,
                       _softplus(sm_ref[:, 2 * LANES:3 * LANES] + pc_ref[1:2, :]), 0.0)
        decay = jnp.exp(dt * (-jnp.exp(pc_ref[0:1, :])))
        xdt_t = (xs * _expand_heads(dt, _expand_matrix())).T
        hi = xdt_t.astype(BF16)
        xt_scr[:, :n_tok] = hi
        xt_scr[:, n_tok:] = (xdt_t - hi.astype(F32)).astype(BF16)
        for h in range(SSM_HEADS):
            a_scr[h] = jnp.broadcast_to(decay[:, h:h + 1], (n_tok, LANES))
        yt_scr[...] = jnp.zeros_like(yt_scr)

    rows = pl.ds(pl.multiple_of(tg * TOK_GROUP, TOK_GROUP), TOK_GROUP)
    bm_blk = bm_scr[rows, :]
    cm_blk = cm_scr[rows, :]
    lane = lax.broadcasted_iota(jnp.int32, (1, LANES), 1)
    piece_tok = lax.broadcasted_iota(jnp.int32, (2 * n_tok, LANES), 0) & (n_tok - 1)
    y_tile = jnp.zeros((SSM_W, LANES), F32)
    for j in range(TOK_GROUP):
        pick = jnp.where(piece_tok == tg * TOK_GROUP + j, 1.0, 0.0).astype(BF16)
        xb = jnp.dot(xt_scr[...], pick, preferred_element_type=F32)
        y_cols = []
        for h in range(SSM_HEADS):
            g = h // SSM_RPG
            s = s_ref[j, h]
            a_row = a_scr[h, rows, :][j:j + 1, :]
            xcol = xb[h * SSM_P:(h + 1) * SSM_P, :]
            s_new = a_row * s + xcol * bm_blk[j:j + 1, g * SSM_N:(g + 1) * SSM_N]
            so_ref[j, h] = s_new
            y_cols.append(jnp.sum(s_new * cm_blk[j:j + 1, g * SSM_N:(g + 1) * SSM_N], axis=1, keepdims=True))
        y_tile = jnp.where(lane == j, jnp.concatenate(y_cols, axis=0), y_tile)
    in_group = (lane >> 3) == tg
    yt_scr[...] = jnp.where(in_group, pltpu.roll(y_tile, TOK_GROUP * tg, axis=1), yt_scr[...])

    @pl.when(tg == pl.num_programs(0) - 1)
    def _():
        xs = xs_scr[...]
        y = yt_scr[...].T + d_ref[...] * xs
        o_ref[...] = _rms(y * _silu(zb_ref[...]), nw_ref[...]).astype(BF16)


def _ssd_sample(proj, sm, buf_t, conv_w, conv_b, pc, d_row, norm_w, state_all, layer, prev_out):
    n_tok = proj.shape[0]
    const = lambda shape: pl.BlockSpec(shape, lambda g: (0,) * len(shape))
    st_spec = pl.BlockSpec((None, TOK_GROUP, SSM_HEADS, SSM_P, SSM_N), lambda g: (layer, g, 0, 0, 0))
    in_specs = [
        pl.BlockSpec((n_tok, SSM_XBC), lambda g: (0, COL_XBC // SSM_XBC)),
        pl.BlockSpec((n_tok, SSM_W), lambda g: (0, COL_ZB // SSM_W)),
        const((n_tok, SMALL_W)), const((3, n_tok, SSM_XBC)), const((4, SSM_XBC)), const((1, SSM_XBC)),
        const((8, LANES)), const((1, SSM_W)), const((1, SSM_W)), st_spec,
    ]
    kern, inputs, in_specs, aliases = _in_place_state(
        _ssd_sample_kernel, [proj, proj, sm, buf_t, conv_w, conv_b, pc, d_row, norm_w, state_all],
        in_specs, prev_out, 1)
    return pl.pallas_call(
        kern,
        grid=(n_tok // TOK_GROUP,),
        in_specs=in_specs,
        out_specs=[const((n_tok, SSM_W)), st_spec, const((3, n_tok, SSM_XBC))],
        out_shape=[_sds((n_tok, SSM_W), BF16), _sds(state_all.shape, F32), _sds((3, n_tok, SSM_XBC), F32)],
        input_output_aliases=aliases,
        scratch_shapes=[pltpu.VMEM((SSM_W, 2 * n_tok), BF16),
                        pltpu.VMEM((n_tok, SSM_W), F32),
                        pltpu.VMEM((n_tok, SSM_GROUPS * SSM_N), F32),
                        pltpu.VMEM((n_tok, SSM_GROUPS * SSM_N), F32),
                        pltpu.VMEM((SSM_HEADS, n_tok, LANES), F32),
                        pltpu.VMEM((SSM_W, n_tok), F32)],
        compiler_params=_params(1),
        name="ssd_sample",
    )(*inputs)


def _cfm_sample_kernel(glu_ref, gc_ref, buf_ref, cw_ref, cb_ref, lnw_ref, lnb_ref, o_ref, bo_ref):
    glu = glu_ref[...]
    u0 = glu[:, :CFM_W] * _sigmoid(glu[:, CFM_W:])
    acc = cw_ref[CFM_K - 1:CFM_K, :] * u0 + cb_ref[...]
    for j in range(CFM_K - 1):
        row = buf_ref[j]
        acc = acc + cw_ref[j:j + 1, :] * row
        if j > 0:
            bo_ref[j - 1] = row
    bo_ref[CFM_K - 2] = u0
    o_ref[...] = _layernorm_silu_gate(acc, gc_ref[...], lnw_ref[...], lnb_ref[...]).astype(BF16)


def _cfm_sample(proj, buf_t, conv_w, conv_b, ln_w, ln_b):
    n_tok = proj.shape[0]
    const = lambda shape: pl.BlockSpec(shape, lambda i: (0,) * len(shape))
    return pl.pallas_call(
        _cfm_sample_kernel,
        grid=(1,),
        in_specs=[
            pl.BlockSpec((n_tok, 2 * CFM_W), lambda i: (0, COL_GLU // (2 * CFM_W))),
            pl.BlockSpec((n_tok, CFM_W), lambda i: (0, COL_GC // CFM_W)),
            const((CFM_K - 1, n_tok, CFM_W)), const((CFM_K, CFM_W)),
            const((1, CFM_W)), const((1, CFM_W)), const((1, CFM_W)),
        ],
        out_specs=[const((n_tok, CFM_W)), const((CFM_K - 1, n_tok, CFM_W))],
        out_shape=[_sds((n_tok, CFM_W), BF16), _sds((CFM_K - 1, n_tok, CFM_W), F32)],
        compiler_params=_params(1),
        name="cfm_sample",
    )(proj, proj, buf_t, conv_w, conv_b, ln_w, ln_b)


_IN_Q, _IN_ZA, _IN_BA, _IN_AA, _IN_ZB, _IN_XBC, _IN_DT, _IN_GLU, _IN_GC, _IN_GATES, _IN_END = (
    0, 1536, 2048, 2052, 2056, 3080, 4616, 4632, 5656, 6168, 9240)


def _prep_w_in(w_in):
    seg = lambda a, b: w_in[:, :, a:b]
    zeros = lambda n: jnp.zeros(w_in.shape[:2] + (n,), w_in.dtype)
    main = jnp.concatenate([seg(_IN_Q, _IN_ZA), seg(_IN_XBC, _IN_DT), seg(_IN_ZB, _IN_XBC),
                            seg(_IN_GLU, _IN_GC), seg(_IN_ZA, _IN_BA), seg(_IN_GC, _IN_GATES),
                            seg(_IN_GATES, _IN_END)], axis=-1).astype(BF16)
    small = jnp.concatenate([seg(_IN_BA, _IN_AA), zeros(LANES - GDN_HEADS),
                             seg(_IN_AA, _IN_ZB), zeros(LANES - GDN_HEADS),
                             seg(_IN_DT, _IN_GLU), zeros(LANES - SSM_HEADS)], axis=-1).astype(BF16)
    small_t = jnp.concatenate([seg(_IN_BA, _IN_AA), zeros(8 - GDN_HEADS),
                               seg(_IN_AA, _IN_ZB), zeros(8 - GDN_HEADS),
                               seg(_IN_DT, _IN_GLU)], axis=-1)
    return main, small, jnp.swapaxes(small_t, 1, 2).astype(BF16)


def _lane_pad(v, rows=8):
    out = jnp.zeros((v[0].shape[0], rows, LANES), F32)
    for i, a in enumerate(v):
        out = out.at[:, i, :a.shape[1]].set(a)
    return out


def _row_rep(v, rows):
    depth, n = v.shape
    out = jnp.zeros((depth, rows, LANES), F32)
    return out.at[:, :n, :].set(jnp.broadcast_to(v[:, :, None], (depth, n, LANES)))


def kernel(x_prompt, x_sample, state_gdn, state_gdn_conv, state_ssm, state_ssm_conv, state_cfm_conv,
           norm_w, w_in, gdn_conv_w, gdn_a_log, gdn_dt_bias, gdn_norm_w, gdn_w_o,
           ssm_conv_w, ssm_conv_b, ssm_a_log, ssm_dt_bias, ssm_d, ssm_norm_w, ssm_w_o,
           cfm_conv_w, cfm_conv_b, cfm_ln_w, cfm_ln_b, cfm_w_o, w_out, final_norm_w):
    depth = w_in.shape[0]
    batch, seq, _ = x_prompt.shape
    n_tok = x_sample.shape[0]

    w_main, w_small, w_small_t = _prep_w_in(w_in)
    wa, wb, wc, wo = (w.astype(BF16) for w in (gdn_w_o, ssm_w_o, cfm_w_o, w_out))
    assert n_tok == LANES and seq % GDN_T == 0
    gdn_pc = _lane_pad([gdn_a_log, gdn_dt_bias])
    ssm_pc = _lane_pad([ssm_a_log, ssm_dt_bias])
    ssm_pr = jnp.concatenate([_row_rep(ssm_a_log, 16), _row_rep(ssm_dt_bias, 16)], axis=1)
    ssm_d_row = jnp.repeat(ssm_d, SSM_P, axis=1)[:, None, :]
    final_w = final_norm_w[None, :]

    hp = x_prompt.reshape(batch * seq, D_MODEL)
    hs = x_sample.reshape(n_tok, D_MODEL)
    p_states, s_states = [], []
    new_gdn_s = new_ssm_s = None
    for i in range(depth):
        final = i == depth - 1
        row = lambda a: a[i][None, :]

        proj, sm, smt = _inproj(hp, row(norm_w), w_main, w_small[i], w_small_t[i], i)
        oa, gdn_s, gdn_cv = _gdn_prompt(proj, sm, gdn_conv_w[i], gdn_pc[i], row(gdn_norm_w), batch, seq)
        yb, ssm_s, ssm_cv = _ssd_prompt(proj, sm, smt, ssm_conv_w[i], row(ssm_conv_b), ssm_pc[i], ssm_pr[i],
                                        ssm_d_row[i], row(ssm_norm_w), batch, seq, t_len=256)
        uc, cfm_cv = _cfm_prompt(proj, cfm_conv_w[i], row(cfm_conv_b), row(cfm_ln_w), row(cfm_ln_b),
                                 batch, seq, t_len=256)
        hp = _out_proj(hp, oa, yb, uc, proj, wa[i], wb[i], wc[i], wo[i], final_w, tm=256, final=final)
        p_states.append((gdn_s, gdn_cv[:, CONV_HIST - 3:], ssm_s, ssm_cv[:, CONV_HIST - 3:],
                         cfm_cv[:, CFM_HIST - (CFM_K - 1):]))

        proj, sm, _ = _inproj(hs, row(norm_w), w_main, w_small[i], w_small_t[i], i)
        oa, new_gdn_s, gdn_buf = _gdn_sample(proj, sm, jnp.swapaxes(state_gdn_conv[i], 0, 1), gdn_conv_w[i],
                                             gdn_pc[i], row(gdn_norm_w), state_gdn, i, new_gdn_s)
        yb, new_ssm_s, ssm_buf = _ssd_sample(proj, sm, jnp.swapaxes(state_ssm_conv[i], 0, 1), ssm_conv_w[i],
                                             row(ssm_conv_b), ssm_pc[i], ssm_d_row[i], row(ssm_norm_w),
                                             state_ssm, i, new_ssm_s)
        uc, cfm_buf = _cfm_sample(proj, jnp.swapaxes(state_cfm_conv[i], 0, 1), cfm_conv_w[i],
                                  row(cfm_conv_b), row(cfm_ln_w), row(cfm_ln_b))
        hs = _out_proj(hs, oa, yb, uc, proj, wa[i], wb[i], wc[i], wo[i], final_w, tm=n_tok, final=final)
        s_states.append((jnp.swapaxes(gdn_buf, 0, 1), jnp.swapaxes(ssm_buf, 0, 1), jnp.swapaxes(cfm_buf, 0, 1)))

    stack = lambda states, j: jnp.stack([s[j] for s in states])
    return (hp.reshape(batch, seq, D_MODEL), hs.reshape(n_tok, 1, D_MODEL),
            stack(p_states, 0), stack(p_states, 1), stack(p_states, 2), stack(p_states, 3), stack(p_states, 4),
            new_gdn_s, stack(s_states, 0), new_ssm_s, stack(s_states, 1), stack(s_states, 2))
```

```python
import functools

import jax
import jax.numpy as jnp
from jax import lax
from jax.experimental import pallas as pl
from jax.experimental.pallas import tpu as pltpu

F32 = jnp.float32
BF16 = jnp.bfloat16

D_MODEL = 1024
GDN_HEADS = 4
GDN_D = 128
GDN_QK_W = GDN_HEADS * GDN_D
GDN_CH = 3 * GDN_QK_W
SSM_W = 1024
SSM_P = 64
SSM_HEADS = 16
SSM_GROUPS = 2
SSM_RPG = SSM_HEADS // SSM_GROUPS
SSM_GW = SSM_RPG * SSM_P
SSM_N = 128
SSM_BC = SSM_GROUPS * SSM_N
SSM_XBC = SSM_W + 2 * SSM_BC
CFM_W = 512
CFM_K = 31
CFM_HIST = 32
CONV_HIST = 8
CHUNK = 64
CHUNK_SHIFT = 6
EPS = 1e-6
NEG = -1e30
LANES = 128

GDN_COL_QKV, GDN_COL_ZA, GDN_COL_BA, GDN_COL_AA, GDN_PROJ_W = 0, 1536, 2048, 2176, 2304
SSD_COL_XBC, SSD_COL_ZB, SSD_COL_DT, SSD_PROJ_W = 0, 1536, 2560, 2688
CFM_COL_GLU, CFM_COL_GC, CFM_PROJ_W = 0, 1024, 1536
GATE_PROJ_W = 3 * D_MODEL

PROMPT_T = 256
VMEM_LIMIT = 56 * 1024 * 1024


def _sds(shape, dtype):
    return jax.ShapeDtypeStruct(shape, dtype)


def _params(n_axes):
    return pltpu.CompilerParams(dimension_semantics=("arbitrary",) * n_axes,
                                vmem_limit_bytes=VMEM_LIMIT)


def _sigmoid(x):
    return jax.nn.sigmoid(x)


def _silu(x):
    return x * jax.nn.sigmoid(x)


def _softplus(x):
    return jnp.maximum(x, 0.0) + jnp.log1p(jnp.exp(-jnp.abs(x)))


def _mm(a, b):
    return jnp.dot(a.astype(BF16), b.astype(BF16), preferred_element_type=F32)


def _mm_nt(a, b):
    return lax.dot_general(a.astype(BF16), b.astype(BF16), (((1,), (1,)), ((), ())),
                           preferred_element_type=F32)


def _mm_tn(a, b):
    return lax.dot_general(a.astype(BF16), b.astype(BF16), (((0,), (0,)), ((), ())),
                           preferred_element_type=F32)


def _rms(x, w):
    return x * lax.rsqrt(jnp.mean(x * x, axis=-1, keepdims=True) + EPS) * w


def _norm_proj(x_ref, nw_ref, w_ref):
    h = _rms(x_ref[...], nw_ref[...]).astype(BF16)
    return jnp.dot(h, w_ref[...], preferred_element_type=F32)


def _iota(shape, dim):
    return lax.broadcasted_iota(jnp.int32, shape, dim)


def _chunk_cumsum(x):
    pos = _iota(x.shape, 0) & (CHUNK - 1)
    d = 1
    while d < CHUNK:
        x = x + jnp.where(pos >= d, pltpu.roll(x, d, axis=0), 0.0)
        d *= 2
    return x


def _chunk_last(x):
    n = x.shape[0] // CHUNK
    return jnp.concatenate(
        [jnp.broadcast_to(x[(c + 1) * CHUNK - 1:(c + 1) * CHUNK, :], (CHUNK, x.shape[1])) for c in range(n)],
        axis=0)


def _expand_heads(x, emat):
    hi = x.astype(BF16).astype(F32)
    r1 = x - hi
    mid = r1.astype(BF16).astype(F32)
    lo = r1 - mid
    packed = hi + pltpu.roll(mid, SSM_HEADS, axis=1) + pltpu.roll(lo, 2 * SSM_HEADS, axis=1)
    return jnp.dot(packed.astype(BF16), emat, preferred_element_type=F32)


def _expand_matrix():
    r = _iota((LANES, SSM_W), 0)
    c = _iota((LANES, SSM_W), 1)
    return jnp.where(((r & (SSM_HEADS - 1)) == (c >> CHUNK_SHIFT)) & (r < 3 * SSM_HEADS), 1.0, 0.0).astype(BF16)


def _causal_conv4(x, xe_scr, w_ref):
    t_len = x.shape[0]
    xe_scr[CONV_HIST:CONV_HIST + t_len, :] = x
    acc = x * w_ref[3:4, :]
    for s in range(1, 4):
        acc = acc + xe_scr[CONV_HIST - s:CONV_HIST - s + t_len, :] * w_ref[3 - s:4 - s, :]
    new_hist = x[t_len - CONV_HIST:, :]
    xe_scr[0:CONV_HIST, :] = new_hist
    return acc, new_hist


GDN_GROUP = PROMPT_T // CHUNK
GDN_CAT = GDN_GROUP * CHUNK
GDN_D_SHIFT = 7


def _lane_cat(x):
    return jnp.concatenate([x[c * CHUNK:(c + 1) * CHUNK, :] for c in range(GDN_GROUP)], axis=1)


def _gdn_stage(proj, cw_ref, pc_ref, nw_ref, st_ref, cv_ref, s_scr, xe_scr, o_scr):
    y, new_hist = _causal_conv4(proj[:, GDN_COL_QKV:GDN_COL_QKV + GDN_CH], xe_scr, cw_ref)
    cv_ref[0] = new_hist
    qkv = _silu(y)

    heads = range(GDN_HEADS)
    q_n, k_n, v_n = [], [], []
    for h in heads:
        q = qkv[:, h * GDN_D:(h + 1) * GDN_D]
        k = qkv[:, GDN_QK_W + h * GDN_D:GDN_QK_W + (h + 1) * GDN_D]
        q_n.append(q * lax.rsqrt(jnp.sum(q * q, axis=-1, keepdims=True) + EPS) * (GDN_D ** -0.5))
        k_n.append(k * lax.rsqrt(jnp.sum(k * k, axis=-1, keepdims=True) + EPS))
        v_n.append(qkv[:, 2 * GDN_QK_W + h * GDN_D:2 * GDN_QK_W + (h + 1) * GDN_D])

    beta_c = _sigmoid(proj[:, GDN_COL_BA:GDN_COL_BA + LANES])
    g_c = -jnp.exp(pc_ref[0:1, :]) * _softplus(proj[:, GDN_COL_AA:GDN_COL_AA + LANES] + pc_ref[1:2, :])
    gam_c = _chunk_cumsum(g_c)
    glast_c = _chunk_last(gam_c)
    eg_c = jnp.exp(gam_c)
    ekd_c = jnp.exp(glast_c - gam_c)
    dtot_c = jnp.exp(glast_c)
    bg_c = beta_c * eg_c
    gam_r = gam_c.T
    col = lambda a, h: a[:, h:h + 1]

    t_i = _iota((CHUNK, GDN_CAT), 0)
    s_i = _iota((CHUNK, GDN_CAT), 1) & (CHUNK - 1)
    incl, strict = t_i >= s_i, t_i > s_i
    eye_cat = jnp.where(t_i == s_i, 1.0, 0.0)
    bd_mask = jnp.where((_iota((GDN_CAT, GDN_CAT), 0) >> CHUNK_SHIFT) == (_iota((GDN_CAT, GDN_CAT), 1) >> CHUNK_SHIFT),
                        1.0, 0.0).astype(BF16)
    k_mask = jnp.where((_iota((PROMPT_T, GDN_GROUP * GDN_D), 0) >> CHUNK_SHIFT)
                       == (_iota((PROMPT_T, GDN_GROUP * GDN_D), 1) >> GDN_D_SHIFT), 1.0, 0.0).astype(BF16)
    low_half = _iota((1, LANES), 1) < CHUNK

    def block_diag(y_cat):
        return jnp.concatenate([y_cat.astype(BF16)] * GDN_GROUP, axis=0) * bd_mask

    kb = [k_n[h] * col(beta_c, h) for h in heads]
    aq = []
    for h in heads:
        rhs_nt = jnp.concatenate([k_n[h].astype(BF16)] * GDN_GROUP, axis=1) * k_mask
        aq.append(_mm_nt(jnp.concatenate([_lane_cat(kb[h]), _lane_cat(q_n[h])], axis=0), rhs_nt))

    dincl = []
    for h in heads:
        gcol = col(gam_c, h)
        tiles = []
        for m in range(GDN_GROUP // 2):
            lo = jnp.broadcast_to(gcol[2 * m * CHUNK:(2 * m + 1) * CHUNK, :], (CHUNK, LANES))
            hi = jnp.broadcast_to(gcol[(2 * m + 1) * CHUNK:(2 * m + 2) * CHUNK, :], (CHUNK, LANES))
            tiles.append(jnp.where(low_half, lo, hi))
        dincl.append(jnp.exp(jnp.where(incl, jnp.concatenate(tiles, axis=1) - gam_r[h:h + 1, :], NEG)))
    n0 = [aq[h][:CHUNK, :] * jnp.where(strict, dincl[h], 0.0) for h in heads]
    attn_bd = [block_diag(aq[h][CHUNK:, :] * dincl[h]) for h in heads]

    p = [eye_cat - n0[h] for h in heads]
    pw = [_mm(n0[h], block_diag(n0[h])) for h in heads]
    for _ in range(4):
        both = [_mm(jnp.concatenate([pw[h], p[h]], axis=0), block_diag(pw[h])) for h in heads]
        pw = [both[h][:CHUNK, :] for h in heads]
        p = [p[h] + both[h][CHUNK:, :] for h in heads]
    tinv = [p[h] + _mm(p[h], block_diag(pw[h])) for h in heads]

    wu = [_mm(block_diag(tinv[h]),
              jnp.concatenate([k_n[h] * col(bg_c, h), v_n[h] * col(beta_c, h)], axis=1)) for h in heads]
    qg = [q_n[h] * col(eg_c, h) for h in heads]
    kd = [k_n[h] * col(ekd_c, h) for h in heads]

    state = [s_scr[h] for h in heads]
    zeros = jnp.zeros((CHUNK, GDN_D), F32)
    for c in range(GDN_GROUP):
        rows = slice(c * CHUNK, (c + 1) * CHUNK)
        ws_qs = [_mm(jnp.concatenate([wu[h][rows, :GDN_D], qg[h][rows, :]], axis=0), state[h]) for h in heads]
        u = [wu[h][rows, GDN_D:] - ws_qs[h][:CHUNK, :] for h in heads]
        for h in heads:
            u_pad = jnp.concatenate([zeros] * c + [u[h]] + [zeros] * (GDN_GROUP - 1 - c), axis=0)
            o_scr[rows, h * GDN_D:(h + 1) * GDN_D] = ws_qs[h][CHUNK:, :] + _mm(attn_bd[h][rows, :], u_pad)
        state = [state[h] * dtot_c[c * CHUNK:c * CHUNK + 1, h:h + 1] + _mm_tn(kd[h][rows, :], u[h])
                 for h in heads]
    for h in heads:
        s_scr[h] = state[h]

    o = o_scr[...]
    za = proj[:, GDN_COL_ZA:GDN_COL_ZA + GDN_QK_W]
    outs = []
    for h in heads:
        sl = slice(h * GDN_D, (h + 1) * GDN_D)
        outs.append(_rms(o[:, sl], nw_ref[...]) * _silu(za[:, sl]))
    st_ref[0] = s_scr[...]
    return jnp.concatenate(outs, axis=1).astype(BF16)


def _ssd_stage(proj, cw_ref, cb_ref, pc_ref, d_ref, nw_ref, st_ref, cv_ref, st_scr, xe_scr, y_scr):
    y, new_hist = _causal_conv4(proj[:, SSD_COL_XBC:SSD_COL_XBC + SSM_XBC], xe_scr, cw_ref)
    cv_ref[0] = new_hist
    xbc = _silu(y + cb_ref[...])
    xs = xbc[:, :SSM_W]
    bm = xbc[:, SSM_W:SSM_W + SSM_BC]
    cm = xbc[:, SSM_W + SSM_BC:]

    head_lane = _iota((1, LANES), 1) < SSM_HEADS
    dt_c = jnp.where(head_lane, _softplus(proj[:, SSD_COL_DT:SSD_COL_DT + LANES] + pc_ref[1:2, :]), 0.0)
    gam_c = _chunk_cumsum(dt_c * (-jnp.exp(pc_ref[0:1, :])))
    glast_c = _chunk_last(gam_c)
    emat = _expand_matrix()
    e1 = _expand_heads(jnp.where(head_lane, jnp.exp(gam_c), 0.0), emat)
    e2 = _expand_heads(jnp.exp(glast_c - gam_c) * dt_c, emat)
    gam_r = gam_c.T
    dt_r = dt_c.T

    t_i = _iota((CHUNK, CHUNK), 0)
    s_i = _iota((CHUNK, CHUNK), 1)
    incl = t_i >= s_i
    low_half = _iota((1, LANES), 1) < SSM_P

    for c in range(PROMPT_T // CHUNK):
        rows = slice(c * CHUNK, (c + 1) * CHUNK)
        xs_c = xs[rows, :]
        bm_c = bm[rows, :]
        cm_c = cm[rows, :]
        gam_cc = gam_c[rows, :]
        cbs = [_mm_nt(cm_c[:, g * SSM_N:(g + 1) * SSM_N], bm_c[:, g * SSM_N:(g + 1) * SSM_N])
               for g in range(SSM_GROUPS)]
        pairs = []
        for j in range(SSM_HEADS // 2):
            xp = xs_c[:, j * LANES:(j + 1) * LANES]
            acc = None
            for half in range(2):
                h = 2 * j + half
                m = (cbs[h // SSM_RPG]
                     * jnp.exp(jnp.where(incl, gam_cc[:, h:h + 1] - gam_r[h:h + 1, rows], NEG))
                     * dt_r[h:h + 1, rows])
                xm = jnp.where(low_half if half == 0 else jnp.logical_not(low_half), xp, 0.0)
                part = _mm(m, xm)
                acc = part if acc is None else acc + part
            pairs.append(acc)
        y_intra = jnp.concatenate(pairs, axis=1)
        y_inter = jnp.concatenate(
            [_mm(cm_c[:, g * SSM_N:(g + 1) * SSM_N], st_scr[g]) for g in range(SSM_GROUPS)], axis=1)
        y_scr[rows, :] = y_intra + y_inter * e1[rows, :] + d_ref[...] * xs_c
        xe = xs_c * e2[rows, :]
        dtot = e1[(c + 1) * CHUNK - 1:(c + 1) * CHUNK, :]
        for g in range(SSM_GROUPS):
            sl = slice(g * SSM_GW, (g + 1) * SSM_GW)
            st_scr[g] = st_scr[g] * dtot[:, sl] + _mm_tn(bm_c[:, g * SSM_N:(g + 1) * SSM_N], xe[:, sl])

    @pl.when(pl.program_id(1) == pl.num_programs(1) - 1)
    def _():
        for g in range(SSM_GROUPS):
            st_ref[0, g * SSM_RPG:(g + 1) * SSM_RPG] = st_scr[g].T.reshape(SSM_RPG, SSM_P, SSM_N)

    zb = proj[:, SSD_COL_ZB:SSD_COL_ZB + SSM_W]
    return _rms(y_scr[...] * _silu(zb), nw_ref[...]).astype(BF16)


def _layernorm_silu_gate(u, gate, lnw, lnb):
    mu = jnp.mean(u, axis=-1, keepdims=True)
    uc = u - mu
    var = jnp.mean(uc * uc, axis=-1, keepdims=True)
    return _silu(uc * lax.rsqrt(var + EPS) * lnw + lnb) * _silu(gate)


def _cfm_stage(proj, cw_ref, cb_ref, lnw_ref, lnb_ref, cv_ref, hist_scr):
    glu = proj[:, CFM_COL_GLU:CFM_COL_GLU + 2 * CFM_W]
    u0 = glu[:, :CFM_W] * _sigmoid(glu[:, CFM_W:])
    xe = jnp.concatenate([hist_scr[...], u0], axis=0)
    acc = u0 * cw_ref[CFM_K - 1:CFM_K, :]
    rolled = {0: xe}
    for s in range(1, CFM_K):
        a, b = divmod(s, 8)
        if b not in rolled:
            rolled[b] = pltpu.roll(xe, b, axis=0)
        lo = CFM_HIST - 8 * a
        acc = acc + rolled[b][lo:lo + PROMPT_T, :] * cw_ref[CFM_K - 1 - s:CFM_K - s, :]
    new_hist = u0[PROMPT_T - CFM_HIST:, :]
    hist_scr[...] = new_hist
    cv_ref[0] = new_hist
    gate = proj[:, CFM_COL_GC:CFM_COL_GC + CFM_W]
    return _layernorm_silu_gate(acc + cb_ref[...], gate, lnw_ref[...], lnb_ref[...]).astype(BF16)


def _merge_stage(x, gates, oa, yb, uc, wa_ref, wb_ref, wc_ref, wo_ref, fw_ref, final):
    dot = functools.partial(jnp.dot, preferred_element_type=F32)
    merged = (_sigmoid(gates[:, :D_MODEL]) * dot(oa, wa_ref[...])
              + _sigmoid(gates[:, D_MODEL:2 * D_MODEL]) * dot(yb, wb_ref[...])
              + _sigmoid(gates[:, 2 * D_MODEL:]) * dot(uc, wc_ref[...]))
    out = x + dot(merged.astype(BF16), wo_ref[...])
    return _rms(out, fw_ref[...]) if final else out


def _prompt_layer_kernel(x_ref, nwin_ref, wg_ref, ws_ref, wc_ref, wgate_ref, wa_ref, wb_ref, wcc_ref, wo_ref, fw_ref,
                         g_cw, g_pc, g_nw, s_cw, s_cb, s_pc, s_d, s_nw, c_cw, c_cb, c_lnw, c_lnb,
                         o_ref, gst_ref, gcv_ref, sst_ref, scv_ref, ccv_ref,
                         s_scr, gxe_scr, o_scr, st_scr, sxe_scr, y_scr, chist_scr, *, final):
    @pl.when(pl.program_id(1) == 0)
    def _():
        s_scr[...] = jnp.zeros_like(s_scr)
        st_scr[...] = jnp.zeros_like(st_scr)
        chist_scr[...] = jnp.zeros_like(chist_scr)
        gxe_scr[0:CONV_HIST, :] = jnp.zeros((CONV_HIST, GDN_CH), F32)
        sxe_scr[0:CONV_HIST, :] = jnp.zeros((CONV_HIST, SSM_XBC), F32)

    dot = functools.partial(jnp.dot, preferred_element_type=F32)
    x = x_ref[...]
    h = _rms(x, nwin_ref[...]).astype(BF16)
    oa = _gdn_stage(dot(h, wg_ref[...]), g_cw, g_pc, g_nw, gst_ref, gcv_ref, s_scr, gxe_scr, o_scr)
    yb = _ssd_stage(dot(h, ws_ref[...]), s_cw, s_cb, s_pc, s_d, s_nw, sst_ref, scv_ref, st_scr, sxe_scr, y_scr)
    uc = _cfm_stage(dot(h, wc_ref[...]), c_cw, c_cb, c_lnw, c_lnb, ccv_ref, chist_scr)
    o_ref[...] = _merge_stage(x, dot(h, wgate_ref[...]), oa, yb, uc, wa_ref, wb_ref, wcc_ref, wo_ref, fw_ref, final)


def _prompt_layer(x, norm_w, weights, layer, final_w, gdn_p, ssd_p, cfm_p, batch, seq, final):
    nt = seq // PROMPT_T
    tok = lambda width: pl.BlockSpec((PROMPT_T, width), lambda b, t: (b * nt + t, 0))
    const = lambda a: pl.BlockSpec(a.shape, lambda b, t: (0,) * a.ndim)
    resident = lambda a: pl.BlockSpec((None,) + a.shape[1:], lambda b, t: (layer,) + (0,) * (a.ndim - 1),
                                      pipeline_mode=pl.Buffered(1))
    per_seq = lambda *shape: pl.BlockSpec((1,) + shape, lambda b, t: (b,) + (0,) * len(shape))
    small = [norm_w] + [final_w] + list(gdn_p) + list(ssd_p) + list(cfm_p)
    inputs = [x, norm_w] + list(weights) + [final_w] + list(gdn_p) + list(ssd_p) + list(cfm_p)
    in_specs = ([tok(D_MODEL), const(norm_w)] + [resident(w) for w in weights]
                + [const(a) for a in small[1:]])
    return pl.pallas_call(
        functools.partial(_prompt_layer_kernel, final=final),
        grid=(batch, nt),
        in_specs=in_specs,
        out_specs=[tok(D_MODEL),
                   per_seq(GDN_HEADS, GDN_D, GDN_D), per_seq(CONV_HIST, GDN_CH),
                   per_seq(SSM_HEADS, SSM_P, SSM_N), per_seq(CONV_HIST, SSM_XBC),
                   per_seq(CFM_HIST, CFM_W)],
        out_shape=[_sds((batch * seq, D_MODEL), F32),
                   _sds((batch, GDN_HEADS, GDN_D, GDN_D), F32), _sds((batch, CONV_HIST, GDN_CH), F32),
                   _sds((batch, SSM_HEADS, SSM_P, SSM_N), F32), _sds((batch, CONV_HIST, SSM_XBC), F32),
                   _sds((batch, CFM_HIST, CFM_W), F32)],
        scratch_shapes=[pltpu.VMEM((GDN_HEADS, GDN_D, GDN_D), F32),
                        pltpu.VMEM((CONV_HIST + PROMPT_T, GDN_CH), F32),
                        pltpu.VMEM((PROMPT_T, GDN_QK_W), F32),
                        pltpu.VMEM((SSM_GROUPS, SSM_N, SSM_GW), F32),
                        pltpu.VMEM((CONV_HIST + PROMPT_T, SSM_XBC), F32),
                        pltpu.VMEM((PROMPT_T, SSM_W), F32),
                        pltpu.VMEM((CFM_HIST, CFM_W), F32)],
        compiler_params=_params(2),
        name="prompt_layer",
    )(*inputs)


def _out_kernel(x_ref, nwin_ref, wg_ref, oa_ref, yb_ref, uc_ref, wa_ref, wb_ref, wc_ref, wo_ref, fw_ref,
                o_ref, *, final):
    x = x_ref[...]
    gates = jnp.dot(_rms(x, nwin_ref[...]).astype(BF16), wg_ref[...], preferred_element_type=F32)
    o_ref[...] = _merge_stage(x, gates, oa_ref[...], yb_ref[...], uc_ref[...],
                              wa_ref, wb_ref, wc_ref, wo_ref, fw_ref, final)


def _out_proj(x, norm_w, w_gate, layer, oa, yb, uc, wa, wb, wc, wo, final_w, final):
    m = x.shape[0]
    const = lambda shape: pl.BlockSpec(shape, lambda i: (0,) * len(shape))
    lw = lambda shape: pl.BlockSpec((None,) + shape, lambda i: (layer,) + (0,) * len(shape))
    return pl.pallas_call(
        functools.partial(_out_kernel, final=final),
        grid=(1,),
        in_specs=[const((m, D_MODEL)), const((1, D_MODEL)), lw((D_MODEL, GATE_PROJ_W)),
                  const((m, GDN_QK_W)), const((m, SSM_W)), const((m, CFM_W)),
                  lw((GDN_QK_W, D_MODEL)), lw((SSM_W, D_MODEL)), lw((CFM_W, D_MODEL)),
                  lw((D_MODEL, D_MODEL)), const((1, D_MODEL))],
        out_specs=const((m, D_MODEL)),
        out_shape=_sds((m, D_MODEL), F32),
        compiler_params=_params(1),
        name="merge_out",
    )(x, norm_w, w_gate, oa, yb, uc, wa, wb, wc, wo, final_w)


SAMPLE_TN = 384


def _proj_kernel(x_ref, nwin_ref, w_ref, o_ref):
    o_ref[...] = _norm_proj(x_ref, nwin_ref, w_ref)


def _sample_proj(x, norm_w, w, layer):
    m = x.shape[0]
    width = w.shape[-1]
    return pl.pallas_call(
        _proj_kernel,
        grid=(width // SAMPLE_TN,),
        in_specs=[pl.BlockSpec((m, D_MODEL), lambda j: (0, 0)),
                  pl.BlockSpec((1, D_MODEL), lambda j: (0, 0)),
                  pl.BlockSpec((None, D_MODEL, SAMPLE_TN), lambda j: (layer, 0, j))],
        out_specs=pl.BlockSpec((m, SAMPLE_TN), lambda j: (0, j)),
        out_shape=_sds((m, width), F32),
        compiler_params=_params(1),
        name="sample_proj",
    )(x, norm_w, w)


TOK_GROUP = 8


def _gdn_sample_kernel(p_ref, buf_ref, cw_ref, pc_ref, nw_ref, s_ref,
                       o_ref, so_ref, bo_ref,
                       kt_scr, qt_scr, v_scr, a_scr, b_scr, qk_scr, o_scr):
    tg = pl.program_id(0)

    @pl.when(tg == 0)
    def _():
        x = p_ref[:, GDN_COL_QKV:GDN_COL_QKV + GDN_CH]
        y = (cw_ref[0:1, :] * buf_ref[0] + cw_ref[1:2, :] * buf_ref[1]
             + cw_ref[2:3, :] * buf_ref[2] + cw_ref[3:4, :] * x)
        bo_ref[0] = buf_ref[1]
        bo_ref[1] = buf_ref[2]
        bo_ref[2] = x
        qkv = _silu(y)
        beta = _sigmoid(p_ref[:, GDN_COL_BA:GDN_COL_BA + LANES])
        decay = jnp.exp(-jnp.exp(pc_ref[0:1, :])
                        * _softplus(p_ref[:, GDN_COL_AA:GDN_COL_AA + LANES] + pc_ref[1:2, :]))
        n_tok = x.shape[0]
        for h in range(GDN_HEADS):
            q = qkv[:, h * GDN_D:(h + 1) * GDN_D]
            k = qkv[:, GDN_QK_W + h * GDN_D:GDN_QK_W + (h + 1) * GDN_D]
            q = q * lax.rsqrt(jnp.sum(q * q, axis=-1, keepdims=True) + EPS) * (GDN_D ** -0.5)
            k = k * lax.rsqrt(jnp.sum(k * k, axis=-1, keepdims=True) + EPS)
            kt_scr[h] = k.T
            qt_scr[h] = q.T
            a_scr[h] = jnp.broadcast_to(decay[:, h:h + 1], (n_tok, LANES))
            b_scr[h] = jnp.broadcast_to(beta[:, h:h + 1], (n_tok, LANES))
            qk_scr[h] = jnp.broadcast_to(jnp.sum(q * k, axis=-1, keepdims=True), (n_tok, LANES))
        v_scr[...] = qkv[:, 2 * GDN_QK_W:]

    shift = (LANES - TOK_GROUP * tg) & (LANES - 1)
    rows = pl.ds(pl.multiple_of(tg * TOK_GROUP, TOK_GROUP), TOK_GROUP)
    for h in range(GDN_HEADS):
        ktg = pltpu.roll(kt_scr[h], shift, axis=1)
        qtg = pltpu.roll(qt_scr[h], shift, axis=1)
        a_blk = a_scr[h, rows, :]
        b_blk = b_scr[h, rows, :]
        qk_blk = qk_scr[h, rows, :]
        v_blk = v_scr[rows, h * GDN_D:(h + 1) * GDN_D]
        o_rows = []
        for j in range(TOK_GROUP):
            s = s_ref[j, h]
            kcol = ktg[:, j:j + 1]
            qcol = qtg[:, j:j + 1]
            ks = jnp.sum(s * kcol, axis=0, keepdims=True)
            qs = jnp.sum(s * qcol, axis=0, keepdims=True)
            a_row = a_blk[j:j + 1, :]
            delta = b_blk[j:j + 1, :] * (v_blk[j:j + 1, :] - a_row * ks)
            so_ref[j, h] = a_row * s + kcol * delta
            o_rows.append(a_row * qs + qk_blk[j:j + 1, :] * delta)
        o_scr[rows, h * GDN_D:(h + 1) * GDN_D] = jnp.concatenate(o_rows, axis=0)

    @pl.when(tg == pl.num_programs(0) - 1)
    def _():
        o = o_scr[...]
        za = p_ref[:, GDN_COL_ZA:GDN_COL_ZA + GDN_QK_W]
        outs = []
        for h in range(GDN_HEADS):
            sl = slice(h * GDN_D, (h + 1) * GDN_D)
            outs.append(_rms(o[:, sl], nw_ref[...]) * _silu(za[:, sl]))
        o_ref[...] = jnp.concatenate(outs, axis=1).astype(BF16)


def _in_place_state(kernel_fn, inputs, in_specs, prev_out, out_index):
    if prev_out is None:
        return kernel_fn, inputs, in_specs, {}
    n = len(inputs)
    wrapped = lambda *refs: kernel_fn(*refs[:n], *refs[n + 1:])
    return wrapped, inputs + [prev_out], in_specs + [pl.BlockSpec(memory_space=pl.ANY)], {n: out_index}


def _gdn_sample(proj, buf_t, conv_w, pc, norm_w, state_all, layer, prev_out):
    n_tok = proj.shape[0]
    const = lambda shape: pl.BlockSpec(shape, lambda g: (0,) * len(shape))
    st_spec = pl.BlockSpec((None, TOK_GROUP, GDN_HEADS, GDN_D, GDN_D), lambda g: (layer, g, 0, 0, 0))
    in_specs = [const((n_tok, GDN_PROJ_W)), const((3, n_tok, GDN_CH)), const((4, GDN_CH)),
                const((8, LANES)), const((1, GDN_D)), st_spec]
    kern, inputs, in_specs, aliases = _in_place_state(
        _gdn_sample_kernel, [proj, buf_t, conv_w, pc, norm_w, state_all], in_specs, prev_out, 1)
    return pl.pallas_call(
        kern,
        grid=(n_tok // TOK_GROUP,),
        in_specs=in_specs,
        out_specs=[const((n_tok, GDN_QK_W)), st_spec, const((3, n_tok, GDN_CH))],
        out_shape=[_sds((n_tok, GDN_QK_W), BF16), _sds(state_all.shape, F32), _sds((3, n_tok, GDN_CH), F32)],
        input_output_aliases=aliases,
        scratch_shapes=[pltpu.VMEM((GDN_HEADS, GDN_D, n_tok), F32),
                        pltpu.VMEM((GDN_HEADS, GDN_D, n_tok), F32),
                        pltpu.VMEM((n_tok, GDN_QK_W), F32),
                        pltpu.VMEM((GDN_HEADS, n_tok, LANES), F32),
                        pltpu.VMEM((GDN_HEADS, n_tok, LANES), F32),
                        pltpu.VMEM((GDN_HEADS, n_tok, LANES), F32),
                        pltpu.VMEM((n_tok, GDN_QK_W), F32)],
        compiler_params=_params(1),
        name="gdn_sample",
    )(*inputs)


def _ssd_sample_kernel(p_ref, buf_ref, cw_ref, cb_ref, pc_ref, d_ref, nw_ref, s_ref,
                       o_ref, so_ref, bo_ref,
                       xt_scr, xs_scr, bm_scr, cm_scr, a_scr, yt_scr):
    tg = pl.program_id(0)
    n_tok = p_ref.shape[0]

    @pl.when(tg == 0)
    def _():
        x = p_ref[:, SSD_COL_XBC:SSD_COL_XBC + SSM_XBC]
        y = (cw_ref[0:1, :] * buf_ref[0] + cw_ref[1:2, :] * buf_ref[1]
             + cw_ref[2:3, :] * buf_ref[2] + cw_ref[3:4, :] * x)
        bo_ref[0] = buf_ref[1]
        bo_ref[1] = buf_ref[2]
        bo_ref[2] = x
        xbc = _silu(y + cb_ref[...])
        xs = xbc[:, :SSM_W]
        xs_scr[...] = xs
        bm_scr[...] = xbc[:, SSM_W:SSM_W + SSM_BC]
        cm_scr[...] = xbc[:, SSM_W + SSM_BC:]
        head_lane = _iota((1, LANES), 1) < SSM_HEADS
        dt = jnp.where(head_lane, _softplus(p_ref[:, SSD_COL_DT:SSD_COL_DT + LANES] + pc_ref[1:2, :]), 0.0)
        decay = jnp.exp(dt * (-jnp.exp(pc_ref[0:1, :])))
        xdt_t = (xs * _expand_heads(dt, _expand_matrix())).T
        hi = xdt_t.astype(BF16)
        xt_scr[:, :n_tok] = hi
        xt_scr[:, n_tok:] = (xdt_t - hi.astype(F32)).astype(BF16)
        for h in range(SSM_HEADS):
            a_scr[h] = jnp.broadcast_to(decay[:, h:h + 1], (n_tok, LANES))
        yt_scr[...] = jnp.zeros_like(yt_scr)

    rows = pl.ds(pl.multiple_of(tg * TOK_GROUP, TOK_GROUP), TOK_GROUP)
    bm_blk = bm_scr[rows, :]
    cm_blk = cm_scr[rows, :]
    lane = _iota((1, LANES), 1)
    piece_tok = _iota((2 * n_tok, LANES), 0) & (n_tok - 1)
    y_tile = jnp.zeros((SSM_W, LANES), F32)
    for j in range(TOK_GROUP):
        pick = jnp.where(piece_tok == tg * TOK_GROUP + j, 1.0, 0.0).astype(BF16)
        xb = jnp.dot(xt_scr[...], pick, preferred_element_type=F32)
        y_cols = []
        for h in range(SSM_HEADS):
            g = h // SSM_RPG
            s = s_ref[j, h]
            a_row = a_scr[h, rows, :][j:j + 1, :]
            xcol = xb[h * SSM_P:(h + 1) * SSM_P, :]
            s_new = a_row * s + xcol * bm_blk[j:j + 1, g * SSM_N:(g + 1) * SSM_N]
            so_ref[j, h] = s_new
            y_cols.append(jnp.sum(s_new * cm_blk[j:j + 1, g * SSM_N:(g + 1) * SSM_N], axis=1, keepdims=True))
        y_tile = jnp.where(lane == j, jnp.concatenate(y_cols, axis=0), y_tile)
    in_group = (lane >> 3) == tg
    yt_scr[...] = jnp.where(in_group, pltpu.roll(y_tile, TOK_GROUP * tg, axis=1), yt_scr[...])

    @pl.when(tg == pl.num_programs(0) - 1)
    def _():
        y = yt_scr[...].T + d_ref[...] * xs_scr[...]
        zb = p_ref[:, SSD_COL_ZB:SSD_COL_ZB + SSM_W]
        o_ref[...] = _rms(y * _silu(zb), nw_ref[...]).astype(BF16)


def _ssd_sample(proj, buf_t, conv_w, conv_b, pc, d_row, norm_w, state_all, layer, prev_out):
    n_tok = proj.shape[0]
    const = lambda shape: pl.BlockSpec(shape, lambda g: (0,) * len(shape))
    st_spec = pl.BlockSpec((None, TOK_GROUP, SSM_HEADS, SSM_P, SSM_N), lambda g: (layer, g, 0, 0, 0))
    in_specs = [const((n_tok, SSD_PROJ_W)), const((3, n_tok, SSM_XBC)), const((4, SSM_XBC)), const((1, SSM_XBC)),
                const((8, LANES)), const((1, SSM_W)), const((1, SSM_W)), st_spec]
    kern, inputs, in_specs, aliases = _in_place_state(
        _ssd_sample_kernel, [proj, buf_t, conv_w, conv_b, pc, d_row, norm_w, state_all], in_specs, prev_out, 1)
    return pl.pallas_call(
        kern,
        grid=(n_tok // TOK_GROUP,),
        in_specs=in_specs,
        out_specs=[const((n_tok, SSM_W)), st_spec, const((3, n_tok, SSM_XBC))],
        out_shape=[_sds((n_tok, SSM_W), BF16), _sds(state_all.shape, F32), _sds((3, n_tok, SSM_XBC), F32)],
        input_output_aliases=aliases,
        scratch_shapes=[pltpu.VMEM((SSM_W, 2 * n_tok), BF16),
                        pltpu.VMEM((n_tok, SSM_W), F32),
                        pltpu.VMEM((n_tok, SSM_BC), F32),
                        pltpu.VMEM((n_tok, SSM_BC), F32),
                        pltpu.VMEM((SSM_HEADS, n_tok, LANES), F32),
                        pltpu.VMEM((SSM_W, n_tok), F32)],
        compiler_params=_params(1),
        name="ssd_sample",
    )(*inputs)


def _cfm_sample_kernel(p_ref, buf_ref, cw_ref, cb_ref, lnw_ref, lnb_ref, o_ref, bo_ref):
    glu = p_ref[:, CFM_COL_GLU:CFM_COL_GLU + 2 * CFM_W]
    u0 = glu[:, :CFM_W] * _sigmoid(glu[:, CFM_W:])
    acc = cw_ref[CFM_K - 1:CFM_K, :] * u0 + cb_ref[...]
    for j in range(CFM_K - 1):
        row = buf_ref[j]
        acc = acc + cw_ref[j:j + 1, :] * row
        if j > 0:
            bo_ref[j - 1] = row
    bo_ref[CFM_K - 2] = u0
    gate = p_ref[:, CFM_COL_GC:CFM_COL_GC + CFM_W]
    o_ref[...] = _layernorm_silu_gate(acc, gate, lnw_ref[...], lnb_ref[...]).astype(BF16)


def _cfm_sample(proj, buf_t, conv_w, conv_b, ln_w, ln_b):
    n_tok = proj.shape[0]
    const = lambda shape: pl.BlockSpec(shape, lambda i: (0,) * len(shape))
    return pl.pallas_call(
        _cfm_sample_kernel,
        grid=(1,),
        in_specs=[const((n_tok, CFM_PROJ_W)), const((CFM_K - 1, n_tok, CFM_W)), const((CFM_K, CFM_W)),
                  const((1, CFM_W)), const((1, CFM_W)), const((1, CFM_W))],
        out_specs=[const((n_tok, CFM_W)), const((CFM_K - 1, n_tok, CFM_W))],
        out_shape=[_sds((n_tok, CFM_W), BF16), _sds((CFM_K - 1, n_tok, CFM_W), F32)],
        compiler_params=_params(1),
        name="cfm_sample",
    )(proj, buf_t, conv_w, conv_b, ln_w, ln_b)


_IN_Q, _IN_ZA, _IN_BA, _IN_AA, _IN_ZB, _IN_XBC, _IN_DT, _IN_GLU, _IN_GC, _IN_GATES, _IN_END = (
    0, 1536, 2048, 2052, 2056, 3080, 4616, 4632, 5656, 6168, 9240)


def _prep_w_in(w_in):
    seg = lambda a, b: w_in[:, :, a:b]
    zeros = lambda n: jnp.zeros(w_in.shape[:2] + (n,), w_in.dtype)
    w_gdn = jnp.concatenate([seg(_IN_Q, _IN_ZA), seg(_IN_ZA, _IN_BA),
                             seg(_IN_BA, _IN_AA), zeros(LANES - GDN_HEADS),
                             seg(_IN_AA, _IN_ZB), zeros(LANES - GDN_HEADS)], axis=-1)
    w_ssd = jnp.concatenate([seg(_IN_XBC, _IN_DT), seg(_IN_ZB, _IN_XBC),
                             seg(_IN_DT, _IN_GLU), zeros(LANES - SSM_HEADS)], axis=-1)
    w_cfm = seg(_IN_GLU, _IN_GATES)
    w_gate = seg(_IN_GATES, _IN_END)
    return tuple(w.astype(BF16) for w in (w_gdn, w_ssd, w_cfm, w_gate))


def _lane_pad(v, rows=8):
    out = jnp.zeros((v[0].shape[0], rows, LANES), F32)
    for i, a in enumerate(v):
        out = out.at[:, i, :a.shape[1]].set(a)
    return out


def kernel(x_prompt, x_sample, state_gdn, state_gdn_conv, state_ssm, state_ssm_conv, state_cfm_conv,
           norm_w, w_in, gdn_conv_w, gdn_a_log, gdn_dt_bias, gdn_norm_w, gdn_w_o,
           ssm_conv_w, ssm_conv_b, ssm_a_log, ssm_dt_bias, ssm_d, ssm_norm_w, ssm_w_o,
           cfm_conv_w, cfm_conv_b, cfm_ln_w, cfm_ln_b, cfm_w_o, w_out, final_norm_w):
    depth = w_in.shape[0]
    batch, seq, _ = x_prompt.shape
    n_tok = x_sample.shape[0]
    assert n_tok == LANES and seq % PROMPT_T == 0

    w_gdn, w_ssd, w_cfm, w_gate = _prep_w_in(w_in)
    wa, wb, wc, wo = (w.astype(BF16) for w in (gdn_w_o, ssm_w_o, cfm_w_o, w_out))
    gdn_pc = _lane_pad([gdn_a_log, gdn_dt_bias])
    ssm_pc = _lane_pad([ssm_a_log, ssm_dt_bias])
    ssm_d_row = jnp.repeat(ssm_d, SSM_P, axis=1)[:, None, :]
    final_w = final_norm_w[None, :]

    hp = x_prompt.reshape(batch * seq, D_MODEL)
    hs = x_sample.reshape(n_tok, D_MODEL)
    p_states, s_states = [], []
    new_gdn_s = new_ssm_s = None
    for i in range(depth):
        final = i == depth - 1
        row = lambda a: a[i][None, :]
        nw = row(norm_w)

        gdn_p = (gdn_conv_w[i], gdn_pc[i], row(gdn_norm_w))
        ssd_p = (ssm_conv_w[i], row(ssm_conv_b), ssm_pc[i], ssm_d_row[i], row(ssm_norm_w))
        cfm_p = (cfm_conv_w[i], row(cfm_conv_b), row(cfm_ln_w), row(cfm_ln_b))
        hp, gdn_s, gdn_cv, ssm_s, ssm_cv, cfm_cv = _prompt_layer(
            hp, nw, (w_gdn, w_ssd, w_cfm, w_gate, wa, wb, wc, wo), i, final_w, gdn_p, ssd_p, cfm_p,
            batch, seq, final)
        p_states.append((gdn_s, gdn_cv[:, CONV_HIST - 3:], ssm_s, ssm_cv[:, CONV_HIST - 3:],
                         cfm_cv[:, CFM_HIST - (CFM_K - 1):]))

        oa, new_gdn_s, gdn_buf = _gdn_sample(_sample_proj(hs, nw, w_gdn, i), jnp.swapaxes(state_gdn_conv[i], 0, 1),
                                             gdn_conv_w[i], gdn_pc[i], row(gdn_norm_w), state_gdn, i, new_gdn_s)
        yb, new_ssm_s, ssm_buf = _ssd_sample(_sample_proj(hs, nw, w_ssd, i), jnp.swapaxes(state_ssm_conv[i], 0, 1),
                                             ssm_conv_w[i], row(ssm_conv_b), ssm_pc[i], ssm_d_row[i],
                                             row(ssm_norm_w), state_ssm, i, new_ssm_s)
        uc, cfm_buf = _cfm_sample(_sample_proj(hs, nw, w_cfm, i), jnp.swapaxes(state_cfm_conv[i], 0, 1),
                                  cfm_conv_w[i], row(cfm_conv_b), row(cfm_ln_w), row(cfm_ln_b))
        hs = _out_proj(hs, nw, w_gate, i, oa, yb, uc, wa, wb, wc, wo, final_w, final)
        s_states.append((jnp.swapaxes(gdn_buf, 0, 1), jnp.swapaxes(ssm_buf, 0, 1), jnp.swapaxes(cfm_buf, 0, 1)))

    stack = lambda states, j: jnp.stack([s[j] for s in states])
    return (hp.reshape(batch, seq, D_MODEL), hs.reshape(n_tok, 1, D_MODEL),
            stack(p_states, 0), stack(p_states, 1), stack(p_states, 2), stack(p_states, 3), stack(p_states, 4),
            new_gdn_s, stack(s_states, 0), new_ssm_s, stack(s_states, 1), stack(s_states, 2))
```

```python
import functools

import jax
import jax.numpy as jnp
from jax import lax
from jax.experimental import pallas as pl
from jax.experimental.pallas import tpu as pltpu

F32 = jnp.float32
BF16 = jnp.bfloat16

D_MODEL = 1024
GDN_HEADS = 4
GDN_D = 128
GDN_QK_W = GDN_HEADS * GDN_D
GDN_CH = 3 * GDN_QK_W
SSM_W = 1024
SSM_P = 64
SSM_HEADS = 16
SSM_GROUPS = 2
SSM_RPG = SSM_HEADS // SSM_GROUPS
SSM_GW = SSM_RPG * SSM_P
SSM_N = 128
SSM_BC = SSM_GROUPS * SSM_N
SSM_XBC = SSM_W + 2 * SSM_BC
CFM_W = 512
CFM_K = 31
CFM_HIST = 32
CONV_HIST = 8
CHUNK = 64
CHUNK_SHIFT = 6
EPS = 1e-6
NEG = -1e30
LANES = 128

GDN_COL_QKV, GDN_COL_ZA, GDN_COL_BA, GDN_COL_AA, GDN_PROJ_W = 0, 1536, 2048, 2176, 2304
SSD_COL_XBC, SSD_COL_ZB, SSD_COL_DT, SSD_PROJ_W = 0, 1536, 2560, 2688
CFM_COL_GLU, CFM_COL_GC, CFM_PROJ_W = 0, 1024, 1536
GATE_PROJ_W = 3 * D_MODEL

PROMPT_T = 256
VMEM_LIMIT = 56 * 1024 * 1024


def _sds(shape, dtype):
    return jax.ShapeDtypeStruct(shape, dtype)


def _params(n_axes):
    return pltpu.CompilerParams(dimension_semantics=("arbitrary",) * n_axes,
                                vmem_limit_bytes=VMEM_LIMIT)


def _sigmoid(x):
    return jax.nn.sigmoid(x)


def _silu(x):
    return x * jax.nn.sigmoid(x)


def _softplus(x):
    return jnp.maximum(x, 0.0) + jnp.log1p(jnp.exp(-jnp.abs(x)))


def _mm(a, b):
    return jnp.dot(a.astype(BF16), b.astype(BF16), preferred_element_type=F32)


def _mm_nt(a, b):
    return lax.dot_general(a.astype(BF16), b.astype(BF16), (((1,), (1,)), ((), ())),
                           preferred_element_type=F32)


def _mm_tn(a, b):
    return lax.dot_general(a.astype(BF16), b.astype(BF16), (((0,), (0,)), ((), ())),
                           preferred_element_type=F32)


def _rms(x, w):
    return x * lax.rsqrt(jnp.mean(x * x, axis=-1, keepdims=True) + EPS) * w


def _norm_proj(x_ref, nw_ref, w_ref):
    h = _rms(x_ref[...], nw_ref[...]).astype(BF16)
    return jnp.dot(h, w_ref[...], preferred_element_type=F32)


def _iota(shape, dim):
    return lax.broadcasted_iota(jnp.int32, shape, dim)


def _chunk_cumsum(x):
    pos = _iota(x.shape, 0) & (CHUNK - 1)
    d = 1
    while d < CHUNK:
        x = x + jnp.where(pos >= d, pltpu.roll(x, d, axis=0), 0.0)
        d *= 2
    return x


def _chunk_last(x):
    n = x.shape[0] // CHUNK
    return jnp.concatenate(
        [jnp.broadcast_to(x[(c + 1) * CHUNK - 1:(c + 1) * CHUNK, :], (CHUNK, x.shape[1])) for c in range(n)],
        axis=0)


def _expand_heads(x, emat):
    hi = x.astype(BF16).astype(F32)
    r1 = x - hi
    mid = r1.astype(BF16).astype(F32)
    lo = r1 - mid
    packed = hi + pltpu.roll(mid, SSM_HEADS, axis=1) + pltpu.roll(lo, 2 * SSM_HEADS, axis=1)
    return jnp.dot(packed.astype(BF16), emat, preferred_element_type=F32)


def _expand_matrix():
    r = _iota((LANES, SSM_W), 0)
    c = _iota((LANES, SSM_W), 1)
    return jnp.where(((r & (SSM_HEADS - 1)) == (c >> CHUNK_SHIFT)) & (r < 3 * SSM_HEADS), 1.0, 0.0).astype(BF16)


def _causal_conv4(x, xe_scr, w_ref):
    t_len = x.shape[0]
    xe = jnp.concatenate([xe_scr[0:CONV_HIST, :], x], axis=0)
    acc = xe * w_ref[0:1, :]
    for j in range(1, 4):
        acc = pltpu.roll(acc, 1, axis=0) + xe * w_ref[j:j + 1, :]
    new_hist = x[t_len - CONV_HIST:, :]
    xe_scr[0:CONV_HIST, :] = new_hist
    return acc[CONV_HIST:, :], new_hist


GDN_GROUP = PROMPT_T // CHUNK
GDN_CAT = GDN_GROUP * CHUNK
GDN_D_SHIFT = 7


def _lane_cat(x):
    return jnp.concatenate([x[c * CHUNK:(c + 1) * CHUNK, :] for c in range(GDN_GROUP)], axis=1)


def _gdn_pre(proj, cw_ref, pc_ref, cv_ref, xe_scr):
    y, new_hist = _causal_conv4(proj[:, GDN_COL_QKV:GDN_COL_QKV + GDN_CH], xe_scr, cw_ref)
    cv_ref[0] = new_hist
    qkv = _silu(y)
    q_n, k_n, v_n = [], [], []
    for h in range(GDN_HEADS):
        q = qkv[:, h * GDN_D:(h + 1) * GDN_D]
        k = qkv[:, GDN_QK_W + h * GDN_D:GDN_QK_W + (h + 1) * GDN_D]
        q_n.append(q * lax.rsqrt(jnp.sum(q * q, axis=-1, keepdims=True) + EPS) * (GDN_D ** -0.5))
        k_n.append(k * lax.rsqrt(jnp.sum(k * k, axis=-1, keepdims=True) + EPS))
        v_n.append(qkv[:, 2 * GDN_QK_W + h * GDN_D:2 * GDN_QK_W + (h + 1) * GDN_D])
    beta_c = _sigmoid(proj[:, GDN_COL_BA:GDN_COL_BA + LANES])
    g_c = -jnp.exp(pc_ref[0:1, :]) * _softplus(proj[:, GDN_COL_AA:GDN_COL_AA + LANES] + pc_ref[1:2, :])
    return q_n, k_n, v_n, beta_c, g_c


def _gdn_main(pre, za, nw_ref, bd_ref, km_ref, st_ref, s_scr, o_scr):
    q_n, k_n, v_n, beta_c, g_c = pre
    heads = range(GDN_HEADS)
    gam_c = _chunk_cumsum(g_c)
    glast_c = _chunk_last(gam_c)
    eg_c = jnp.exp(gam_c)
    ekd_c = jnp.exp(glast_c - gam_c)
    dtot_c = jnp.exp(glast_c)
    bg_c = beta_c * eg_c
    gam_r = gam_c.T
    col = lambda a, h: a[:, h:h + 1]

    t_i = _iota((CHUNK, GDN_CAT), 0)
    s_i = _iota((CHUNK, GDN_CAT), 1) & (CHUNK - 1)
    incl, strict = t_i >= s_i, t_i > s_i
    eye_cat = jnp.where(t_i == s_i, 1.0, 0.0)
    bd_mask = bd_ref[...]
    k_mask = km_ref[...]
    low_half = _iota((1, LANES), 1) < CHUNK

    def block_diag(y_cat):
        return jnp.concatenate([y_cat.astype(BF16)] * GDN_GROUP, axis=0) * bd_mask

    kb = [k_n[h] * col(beta_c, h) for h in heads]
    aq = []
    for h in heads:
        rhs_nt = jnp.concatenate([k_n[h].astype(BF16)] * GDN_GROUP, axis=1) * k_mask
        aq.append(_mm_nt(jnp.concatenate([_lane_cat(kb[h]), _lane_cat(q_n[h])], axis=0), rhs_nt))

    dincl = []
    for h in heads:
        gcol = col(gam_c, h)
        tiles = []
        for m in range(GDN_GROUP // 2):
            lo = jnp.broadcast_to(gcol[2 * m * CHUNK:(2 * m + 1) * CHUNK, :], (CHUNK, LANES))
            hi = jnp.broadcast_to(gcol[(2 * m + 1) * CHUNK:(2 * m + 2) * CHUNK, :], (CHUNK, LANES))
            tiles.append(jnp.where(low_half, lo, hi))
        dincl.append(jnp.exp(jnp.where(incl, jnp.concatenate(tiles, axis=1) - gam_r[h:h + 1, :], NEG)))
    n0 = [aq[h][:CHUNK, :] * jnp.where(strict, dincl[h], 0.0) for h in heads]
    attn_bd = [block_diag(aq[h][CHUNK:, :] * dincl[h]) for h in heads]

    p = [eye_cat - n0[h] for h in heads]
    pw = [_mm(n0[h], block_diag(n0[h])) for h in heads]
    for _ in range(4):
        both = [_mm(jnp.concatenate([pw[h], p[h]], axis=0), block_diag(pw[h])) for h in heads]
        pw = [both[h][:CHUNK, :] for h in heads]
        p = [p[h] + both[h][CHUNK:, :] for h in heads]
    tinv = [p[h] + _mm(p[h], block_diag(pw[h])) for h in heads]

    wu = [_mm(block_diag(tinv[h]),
              jnp.concatenate([k_n[h] * col(bg_c, h), v_n[h] * col(beta_c, h)], axis=1)) for h in heads]
    qg = [q_n[h] * col(eg_c, h) for h in heads]
    kd = [k_n[h] * col(ekd_c, h) for h in heads]

    state = [s_scr[h] for h in heads]
    zeros = jnp.zeros((CHUNK, GDN_D), F32)
    for c in range(GDN_GROUP):
        rows = slice(c * CHUNK, (c + 1) * CHUNK)
        ws_qs = [_mm(jnp.concatenate([wu[h][rows, :GDN_D], qg[h][rows, :]], axis=0), state[h]) for h in heads]
        u = [wu[h][rows, GDN_D:] - ws_qs[h][:CHUNK, :] for h in heads]
        for h in heads:
            u_pad = jnp.concatenate([zeros] * c + [u[h]] + [zeros] * (GDN_GROUP - 1 - c), axis=0)
            o_scr[rows, h * GDN_D:(h + 1) * GDN_D] = ws_qs[h][CHUNK:, :] + _mm(attn_bd[h][rows, :], u_pad)
        state = [state[h] * dtot_c[c * CHUNK:c * CHUNK + 1, h:h + 1] + _mm_tn(kd[h][rows, :], u[h])
                 for h in heads]
    for h in heads:
        s_scr[h] = state[h]

    o = o_scr[...]
    outs = []
    for h in heads:
        sl = slice(h * GDN_D, (h + 1) * GDN_D)
        outs.append(_rms(o[:, sl], nw_ref[...]) * _silu(za[:, sl]))
    st_ref[0] = s_scr[...]
    return jnp.concatenate(outs, axis=1).astype(BF16)


def _ssd_pre(proj, cw_ref, cb_ref, pc_ref, cv_ref, xe_scr):
    y, new_hist = _causal_conv4(proj[:, SSD_COL_XBC:SSD_COL_XBC + SSM_XBC], xe_scr, cw_ref)
    cv_ref[0] = new_hist
    xbc = _silu(y + cb_ref[...])
    head_lane = _iota((1, LANES), 1) < SSM_HEADS
    dt_c = jnp.where(head_lane, _softplus(proj[:, SSD_COL_DT:SSD_COL_DT + LANES] + pc_ref[1:2, :]), 0.0)
    return xbc, dt_c


def _ssd_main(pre, zb, pc_ref, d_ref, nw_ref, emat_ref, st_ref, st_scr, y_scr):
    xbc, dt_c = pre
    xs = xbc[:, :SSM_W]
    bm = xbc[:, SSM_W:SSM_W + SSM_BC]
    cm = xbc[:, SSM_W + SSM_BC:]
    head_lane = _iota((1, LANES), 1) < SSM_HEADS
    gam_c = _chunk_cumsum(dt_c * (-jnp.exp(pc_ref[0:1, :])))
    glast_c = _chunk_last(gam_c)
    emat = emat_ref[...]
    e1 = _expand_heads(jnp.where(head_lane, jnp.exp(gam_c), 0.0), emat)
    e2 = _expand_heads(jnp.exp(glast_c - gam_c) * dt_c, emat)
    gam_r = gam_c.T
    dt_r = dt_c.T

    t_i = _iota((CHUNK, CHUNK), 0)
    s_i = _iota((CHUNK, CHUNK), 1)
    incl = t_i >= s_i
    low_half = _iota((1, LANES), 1) < SSM_P

    for c in range(PROMPT_T // CHUNK):
        rows = slice(c * CHUNK, (c + 1) * CHUNK)
        xs_c = xs[rows, :]
        bm_c = bm[rows, :]
        cm_c = cm[rows, :]
        gam_cc = gam_c[rows, :]
        cbs = [_mm_nt(cm_c[:, g * SSM_N:(g + 1) * SSM_N], bm_c[:, g * SSM_N:(g + 1) * SSM_N])
               for g in range(SSM_GROUPS)]
        pairs = []
        for j in range(SSM_HEADS // 2):
            xp = xs_c[:, j * LANES:(j + 1) * LANES]
            acc = None
            for half in range(2):
                h = 2 * j + half
                m = (cbs[h // SSM_RPG]
                     * jnp.exp(jnp.where(incl, gam_cc[:, h:h + 1] - gam_r[h:h + 1, rows], NEG))
                     * dt_r[h:h + 1, rows])
                xm = jnp.where(low_half if half == 0 else jnp.logical_not(low_half), xp, 0.0)
                part = _mm(m, xm)
                acc = part if acc is None else acc + part
            pairs.append(acc)
        y_intra = jnp.concatenate(pairs, axis=1)
        y_inter = jnp.concatenate(
            [_mm(cm_c[:, g * SSM_N:(g + 1) * SSM_N], st_scr[g]) for g in range(SSM_GROUPS)], axis=1)
        y_scr[rows, :] = y_intra + y_inter * e1[rows, :] + d_ref[...] * xs_c
        xe = xs_c * e2[rows, :]
        dtot = e1[(c + 1) * CHUNK - 1:(c + 1) * CHUNK, :]
        for g in range(SSM_GROUPS):
            sl = slice(g * SSM_GW, (g + 1) * SSM_GW)
            st_scr[g] = st_scr[g] * dtot[:, sl] + _mm_tn(bm_c[:, g * SSM_N:(g + 1) * SSM_N], xe[:, sl])

    @pl.when(pl.program_id(1) == pl.num_programs(1) - 1)
    def _():
        for g in range(SSM_GROUPS):
            st_ref[0, g * SSM_RPG:(g + 1) * SSM_RPG] = st_scr[g].T.reshape(SSM_RPG, SSM_P, SSM_N)

    return _rms(y_scr[...] * _silu(zb), nw_ref[...]).astype(BF16)


def _layernorm_silu_gate(u, gate, lnw, lnb):
    mu = jnp.mean(u, axis=-1, keepdims=True)
    uc = u - mu
    var = jnp.mean(uc * uc, axis=-1, keepdims=True)
    return _silu(uc * lax.rsqrt(var + EPS) * lnw + lnb) * _silu(gate)


def _cfm_stage(proj, cw_ref, cb_ref, lnw_ref, lnb_ref, cv_ref, hist_scr):
    glu = proj[:, CFM_COL_GLU:CFM_COL_GLU + 2 * CFM_W]
    u0 = glu[:, :CFM_W] * _sigmoid(glu[:, CFM_W:])
    xe = jnp.concatenate([hist_scr[...], u0], axis=0)
    acc = u0 * cw_ref[CFM_K - 1:CFM_K, :]
    rolled = {0: xe}
    for s in range(1, CFM_K):
        a, b = divmod(s, 8)
        if b not in rolled:
            rolled[b] = pltpu.roll(xe, b, axis=0)
        lo = CFM_HIST - 8 * a
        acc = acc + rolled[b][lo:lo + PROMPT_T, :] * cw_ref[CFM_K - 1 - s:CFM_K - s, :]
    new_hist = u0[PROMPT_T - CFM_HIST:, :]
    hist_scr[...] = new_hist
    cv_ref[0] = new_hist
    gate = proj[:, CFM_COL_GC:CFM_COL_GC + CFM_W]
    return _layernorm_silu_gate(acc + cb_ref[...], gate, lnw_ref[...], lnb_ref[...]).astype(BF16)


def _merge_stage(x, gates, oa, yb, uc, wa_ref, wb_ref, wc_ref, wo_ref, fw_ref, final):
    dot = functools.partial(jnp.dot, preferred_element_type=F32)
    merged = (_sigmoid(gates[:, :D_MODEL]) * dot(oa, wa_ref[...])
              + _sigmoid(gates[:, D_MODEL:2 * D_MODEL]) * dot(yb, wb_ref[...])
              + _sigmoid(gates[:, 2 * D_MODEL:]) * dot(uc, wc_ref[...]))
    out = x + dot(merged.astype(BF16), wo_ref[...])
    return _rms(out, fw_ref[...]) if final else out


def _prompt_layer_kernel(x_ref, nwin_ref, wg_ref, ws_ref, wc_ref, wgate_ref, wa_ref, wb_ref, wcc_ref, wo_ref, fw_ref,
                         g_cw, g_pc, g_nw, s_cw, s_cb, s_pc, s_d, s_nw, c_cw, c_cb, c_lnw, c_lnb,
                         bd_ref, km_ref, emat_ref,
                         o_ref, gst_ref, gcv_ref, sst_ref, scv_ref, ccv_ref,
                         s_scr, gxe_scr, o_scr, st_scr, sxe_scr, y_scr, chist_scr, *, final):
    @pl.when(pl.program_id(1) == 0)
    def _():
        s_scr[...] = jnp.zeros_like(s_scr)
        st_scr[...] = jnp.zeros_like(st_scr)
        chist_scr[...] = jnp.zeros_like(chist_scr)
        gxe_scr[0:CONV_HIST, :] = jnp.zeros((CONV_HIST, GDN_CH), F32)
        sxe_scr[0:CONV_HIST, :] = jnp.zeros((CONV_HIST, SSM_XBC), F32)

    dot = functools.partial(jnp.dot, preferred_element_type=F32)
    x = x_ref[...]
    h = _rms(x, nwin_ref[...]).astype(BF16)
    pc = dot(h, wc_ref[...])
    pg = dot(h, wg_ref[...])
    uc = _cfm_stage(pc, c_cw, c_cb, c_lnw, c_lnb, ccv_ref, chist_scr)
    ps = dot(h, ws_ref[...])
    gdn_pre = _gdn_pre(pg, g_cw, g_pc, gcv_ref, gxe_scr)
    pgate = dot(h, wgate_ref[...])
    ssd_pre = _ssd_pre(ps, s_cw, s_cb, s_pc, scv_ref, sxe_scr)
    oa = _gdn_main(gdn_pre, pg[:, GDN_COL_ZA:GDN_COL_ZA + GDN_QK_W], g_nw, bd_ref, km_ref, gst_ref, s_scr, o_scr)
    yb = _ssd_main(ssd_pre, ps[:, SSD_COL_ZB:SSD_COL_ZB + SSM_W], s_pc, s_d, s_nw, emat_ref, sst_ref, st_scr,
                   y_scr)
    o_ref[...] = _merge_stage(x, pgate, oa, yb, uc, wa_ref, wb_ref, wcc_ref, wo_ref, fw_ref, final)


def _prompt_layer(x, norm_w, weights, layer, final_w, gdn_p, ssd_p, cfm_p, batch, seq, final):
    nt = seq // PROMPT_T
    tok = lambda width: pl.BlockSpec((PROMPT_T, width), lambda b, t: (b * nt + t, 0))
    const = lambda a: pl.BlockSpec(a.shape, lambda b, t: (0,) * a.ndim)
    resident = lambda a: pl.BlockSpec((None,) + a.shape[1:], lambda b, t: (layer,) + (0,) * (a.ndim - 1),
                                      pipeline_mode=pl.Buffered(1))
    per_seq = lambda *shape: pl.BlockSpec((1,) + shape, lambda b, t: (b,) + (0,) * len(shape))
    chunk_of = jnp.arange(GDN_CAT) // CHUNK
    bd_mask = (chunk_of[:, None] == chunk_of[None, :]).astype(BF16)
    k_mask = (chunk_of[:, None] == (jnp.arange(GDN_GROUP * GDN_D) // GDN_D)[None, :]).astype(BF16)
    small = [final_w] + list(gdn_p) + list(ssd_p) + list(cfm_p) + [bd_mask, k_mask, _expand_matrix()]
    inputs = [x, norm_w] + list(weights) + small
    in_specs = ([tok(D_MODEL), const(norm_w)] + [resident(w) for w in weights] + [const(a) for a in small])
    return pl.pallas_call(
        functools.partial(_prompt_layer_kernel, final=final),
        grid=(batch, nt),
        in_specs=in_specs,
        out_specs=[tok(D_MODEL),
                   per_seq(GDN_HEADS, GDN_D, GDN_D), per_seq(CONV_HIST, GDN_CH),
                   per_seq(SSM_HEADS, SSM_P, SSM_N), per_seq(CONV_HIST, SSM_XBC),
                   per_seq(CFM_HIST, CFM_W)],
        out_shape=[_sds((batch * seq, D_MODEL), F32),
                   _sds((batch, GDN_HEADS, GDN_D, GDN_D), F32), _sds((batch, CONV_HIST, GDN_CH), F32),
                   _sds((batch, SSM_HEADS, SSM_P, SSM_N), F32), _sds((batch, CONV_HIST, SSM_XBC), F32),
                   _sds((batch, CFM_HIST, CFM_W), F32)],
        scratch_shapes=[pltpu.VMEM((GDN_HEADS, GDN_D, GDN_D), F32),
                        pltpu.VMEM((CONV_HIST + PROMPT_T, GDN_CH), F32),
                        pltpu.VMEM((PROMPT_T, GDN_QK_W), F32),
                        pltpu.VMEM((SSM_GROUPS, SSM_N, SSM_GW), F32),
                        pltpu.VMEM((CONV_HIST + PROMPT_T, SSM_XBC), F32),
                        pltpu.VMEM((PROMPT_T, SSM_W), F32),
                        pltpu.VMEM((CFM_HIST, CFM_W), F32)],
        compiler_params=_params(2),
        name="prompt_layer",
    )(*inputs)


def _out_kernel(x_ref, nwin_ref, wg_ref, oa_ref, yb_ref, uc_ref, wa_ref, wb_ref, wc_ref, wo_ref, fw_ref,
                o_ref, *, final):
    x = x_ref[...]
    gates = jnp.dot(_rms(x, nwin_ref[...]).astype(BF16), wg_ref[...], preferred_element_type=F32)
    o_ref[...] = _merge_stage(x, gates, oa_ref[...], yb_ref[...], uc_ref[...],
                              wa_ref, wb_ref, wc_ref, wo_ref, fw_ref, final)


def _out_proj(x, norm_w, w_gate, layer, oa, yb, uc, wa, wb, wc, wo, final_w, final):
    m = x.shape[0]
    const = lambda shape: pl.BlockSpec(shape, lambda i: (0,) * len(shape))
    lw = lambda shape: pl.BlockSpec((None,) + shape, lambda i: (layer,) + (0,) * len(shape))
    return pl.pallas_call(
        functools.partial(_out_kernel, final=final),
        grid=(1,),
        in_specs=[const((m, D_MODEL)), const((1, D_MODEL)), lw((D_MODEL, GATE_PROJ_W)),
                  const((m, GDN_QK_W)), const((m, SSM_W)), const((m, CFM_W)),
                  lw((GDN_QK_W, D_MODEL)), lw((SSM_W, D_MODEL)), lw((CFM_W, D_MODEL)),
                  lw((D_MODEL, D_MODEL)), const((1, D_MODEL))],
        out_specs=const((m, D_MODEL)),
        out_shape=_sds((m, D_MODEL), F32),
        compiler_params=_params(1),
        name="merge_out",
    )(x, norm_w, w_gate, oa, yb, uc, wa, wb, wc, wo, final_w)


SAMPLE_TN = 384


def _proj_kernel(x_ref, nwin_ref, w_ref, o_ref):
    o_ref[...] = _norm_proj(x_ref, nwin_ref, w_ref)


def _sample_proj(x, norm_w, w, layer):
    m = x.shape[0]
    width = w.shape[-1]
    return pl.pallas_call(
        _proj_kernel,
        grid=(width // SAMPLE_TN,),
        in_specs=[pl.BlockSpec((m, D_MODEL), lambda j: (0, 0)),
                  pl.BlockSpec((1, D_MODEL), lambda j: (0, 0)),
                  pl.BlockSpec((None, D_MODEL, SAMPLE_TN), lambda j: (layer, 0, j))],
        out_specs=pl.BlockSpec((m, SAMPLE_TN), lambda j: (0, j)),
        out_shape=_sds((m, width), F32),
        compiler_params=_params(1),
        name="sample_proj",
    )(x, norm_w, w)


TOK_GROUP = 8


def _gdn_sample_kernel(p_ref, buf_ref, cw_ref, pc_ref, nw_ref, s_ref,
                       o_ref, so_ref, bo_ref,
                       kt_scr, qt_scr, v_scr, a_scr, b_scr, qk_scr, o_scr):
    tg = pl.program_id(0)

    @pl.when(tg == 0)
    def _():
        x = p_ref[:, GDN_COL_QKV:GDN_COL_QKV + GDN_CH]
        y = (cw_ref[0:1, :] * buf_ref[0] + cw_ref[1:2, :] * buf_ref[1]
             + cw_ref[2:3, :] * buf_ref[2] + cw_ref[3:4, :] * x)
        bo_ref[0] = buf_ref[1]
        bo_ref[1] = buf_ref[2]
        bo_ref[2] = x
        qkv = _silu(y)
        beta = _sigmoid(p_ref[:, GDN_COL_BA:GDN_COL_BA + LANES])
        decay = jnp.exp(-jnp.exp(pc_ref[0:1, :])
                        * _softplus(p_ref[:, GDN_COL_AA:GDN_COL_AA + LANES] + pc_ref[1:2, :]))
        n_tok = x.shape[0]
        for h in range(GDN_HEADS):
            q = qkv[:, h * GDN_D:(h + 1) * GDN_D]
            k = qkv[:, GDN_QK_W + h * GDN_D:GDN_QK_W + (h + 1) * GDN_D]
            q = q * lax.rsqrt(jnp.sum(q * q, axis=-1, keepdims=True) + EPS) * (GDN_D ** -0.5)
            k = k * lax.rsqrt(jnp.sum(k * k, axis=-1, keepdims=True) + EPS)
            kt_scr[h] = k.T
            qt_scr[h] = q.T
            a_scr[h] = jnp.broadcast_to(decay[:, h:h + 1], (n_tok, LANES))
            b_scr[h] = jnp.broadcast_to(beta[:, h:h + 1], (n_tok, LANES))
            qk_scr[h] = jnp.broadcast_to(jnp.sum(q * k, axis=-1, keepdims=True), (n_tok, LANES))
        v_scr[...] = qkv[:, 2 * GDN_QK_W:]

    shift = (LANES - TOK_GROUP * tg) & (LANES - 1)
    rows = pl.ds(pl.multiple_of(tg * TOK_GROUP, TOK_GROUP), TOK_GROUP)
    for h in range(GDN_HEADS):
        ktg = pltpu.roll(kt_scr[h], shift, axis=1)
        qtg = pltpu.roll(qt_scr[h], shift, axis=1)
        a_blk = a_scr[h, rows, :]
        b_blk = b_scr[h, rows, :]
        qk_blk = qk_scr[h, rows, :]
        v_blk = v_scr[rows, h * GDN_D:(h + 1) * GDN_D]
        o_rows = []
        for j in range(TOK_GROUP):
            s = s_ref[j, h]
            kcol = ktg[:, j:j + 1]
            qcol = qtg[:, j:j + 1]
            ks = jnp.sum(s * kcol, axis=0, keepdims=True)
            qs = jnp.sum(s * qcol, axis=0, keepdims=True)
            a_row = a_blk[j:j + 1, :]
            delta = b_blk[j:j + 1, :] * (v_blk[j:j + 1, :] - a_row * ks)
            so_ref[j, h] = a_row * s + kcol * delta
            o_rows.append(a_row * qs + qk_blk[j:j + 1, :] * delta)
        o_scr[rows, h * GDN_D:(h + 1) * GDN_D] = jnp.concatenate(o_rows, axis=0)

    @pl.when(tg == pl.num_programs(0) - 1)
    def _():
        o = o_scr[...]
        za = p_ref[:, GDN_COL_ZA:GDN_COL_ZA + GDN_QK_W]
        outs = []
        for h in range(GDN_HEADS):
            sl = slice(h * GDN_D, (h + 1) * GDN_D)
            outs.append(_rms(o[:, sl], nw_ref[...]) * _silu(za[:, sl]))
        o_ref[...] = jnp.concatenate(outs, axis=1).astype(BF16)


def _in_place_state(kernel_fn, inputs, in_specs, prev_out, out_index):
    if prev_out is None:
        return kernel_fn, inputs, in_specs, {}
    n = len(inputs)
    wrapped = lambda *refs: kernel_fn(*refs[:n], *refs[n + 1:])
    return wrapped, inputs + [prev_out], in_specs + [pl.BlockSpec(memory_space=pl.ANY)], {n: out_index}


def _gdn_sample(proj, buf_t, conv_w, pc, norm_w, state_all, layer, prev_out):
    n_tok = proj.shape[0]
    const = lambda shape: pl.BlockSpec(shape, lambda g: (0,) * len(shape))
    st_spec = pl.BlockSpec((None, TOK_GROUP, GDN_HEADS, GDN_D, GDN_D), lambda g: (layer, g, 0, 0, 0))
    in_specs = [const((n_tok, GDN_PROJ_W)), const((3, n_tok, GDN_CH)), const((4, GDN_CH)),
                const((8, LANES)), const((1, GDN_D)), st_spec]
    kern, inputs, in_specs, aliases = _in_place_state(
        _gdn_sample_kernel, [proj, buf_t, conv_w, pc, norm_w, state_all], in_specs, prev_out, 1)
    return pl.pallas_call(
        kern,
        grid=(n_tok // TOK_GROUP,),
        in_specs=in_specs,
        out_specs=[const((n_tok, GDN_QK_W)), st_spec, const((3, n_tok, GDN_CH))],
        out_shape=[_sds((n_tok, GDN_QK_W), BF16), _sds(state_all.shape, F32), _sds((3, n_tok, GDN_CH), F32)],
        input_output_aliases=aliases,
        scratch_shapes=[pltpu.VMEM((GDN_HEADS, GDN_D, n_tok), F32),
                        pltpu.VMEM((GDN_HEADS, GDN_D, n_tok), F32),
                        pltpu.VMEM((n_tok, GDN_QK_W), F32),
                        pltpu.VMEM((GDN_HEADS, n_tok, LANES), F32),
                        pltpu.VMEM((GDN_HEADS, n_tok, LANES), F32),
                        pltpu.VMEM((GDN_HEADS, n_tok, LANES), F32),
                        pltpu.VMEM((n_tok, GDN_QK_W), F32)],
        compiler_params=_params(1),
        name="gdn_sample",
    )(*inputs)


def _ssd_sample_kernel(p_ref, buf_ref, cw_ref, cb_ref, pc_ref, d_ref, nw_ref, s_ref,
                       o_ref, so_ref, bo_ref,
                       xt_scr, xs_scr, bm_scr, cm_scr, a_scr, yt_scr):
    tg = pl.program_id(0)
    n_tok = p_ref.shape[0]

    @pl.when(tg == 0)
    def _():
        x = p_ref[:, SSD_COL_XBC:SSD_COL_XBC + SSM_XBC]
        y = (cw_ref[0:1, :] * buf_ref[0] + cw_ref[1:2, :] * buf_ref[1]
             + cw_ref[2:3, :] * buf_ref[2] + cw_ref[3:4, :] * x)
        bo_ref[0] = buf_ref[1]
        bo_ref[1] = buf_ref[2]
        bo_ref[2] = x
        xbc = _silu(y + cb_ref[...])
        xs = xbc[:, :SSM_W]
        xs_scr[...] = xs
        bm_scr[...] = xbc[:, SSM_W:SSM_W + SSM_BC]
        cm_scr[...] = xbc[:, SSM_W + SSM_BC:]
        head_lane = _iota((1, LANES), 1) < SSM_HEADS
        dt = jnp.where(head_lane, _softplus(p_ref[:, SSD_COL_DT:SSD_COL_DT + LANES] + pc_ref[1:2, :]), 0.0)
        decay = jnp.exp(dt * (-jnp.exp(pc_ref[0:1, :])))
        xdt_t = (xs * _expand_heads(dt, _expand_matrix())).T
        hi = xdt_t.astype(BF16)
        xt_scr[:, :n_tok] = hi
        xt_scr[:, n_tok:] = (xdt_t - hi.astype(F32)).astype(BF16)
        for h in range(SSM_HEADS):
            a_scr[h] = jnp.broadcast_to(decay[:, h:h + 1], (n_tok, LANES))
        yt_scr[...] = jnp.zeros_like(yt_scr)

    rows = pl.ds(pl.multiple_of(tg * TOK_GROUP, TOK_GROUP), TOK_GROUP)
    bm_blk = bm_scr[rows, :]
    cm_blk = cm_scr[rows, :]
    lane = _iota((1, LANES), 1)
    piece_tok = _iota((2 * n_tok, LANES), 0) & (n_tok - 1)
    y_tile = jnp.zeros((SSM_W, LANES), F32)
    for j in range(TOK_GROUP):
        pick = jnp.where(piece_tok == tg * TOK_GROUP + j, 1.0, 0.0).astype(BF16)
        xb = jnp.dot(xt_scr[...], pick, preferred_element_type=F32)
        y_cols = []
        for h in range(SSM_HEADS):
            g = h // SSM_RPG
            s = s_ref[j, h]
            a_row = a_scr[h, rows, :][j:j + 1, :]
            xcol = xb[h * SSM_P:(h + 1) * SSM_P, :]
            s_new = a_row * s + xcol * bm_blk[j:j + 1, g * SSM_N:(g + 1) * SSM_N]
            so_ref[j, h] = s_new
            y_cols.append(jnp.sum(s_new * cm_blk[j:j + 1, g * SSM_N:(g + 1) * SSM_N], axis=1, keepdims=True))
        y_tile = jnp.where(lane == j, jnp.concatenate(y_cols, axis=0), y_tile)
    in_group = (lane >> 3) == tg
    yt_scr[...] = jnp.where(in_group, pltpu.roll(y_tile, TOK_GROUP * tg, axis=1), yt_scr[...])

    @pl.when(tg == pl.num_programs(0) - 1)
    def _():
        y = yt_scr[...].T + d_ref[...] * xs_scr[...]
        zb = p_ref[:, SSD_COL_ZB:SSD_COL_ZB + SSM_W]
        o_ref[...] = _rms(y * _silu(zb), nw_ref[...]).astype(BF16)


def _ssd_sample(proj, buf_t, conv_w, conv_b, pc, d_row, norm_w, state_all, layer, prev_out):
    n_tok = proj.shape[0]
    const = lambda shape: pl.BlockSpec(shape, lambda g: (0,) * len(shape))
    st_spec = pl.BlockSpec((None, TOK_GROUP, SSM_HEADS, SSM_P, SSM_N), lambda g: (layer, g, 0, 0, 0))
    in_specs = [const((n_tok, SSD_PROJ_W)), const((3, n_tok, SSM_XBC)), const((4, SSM_XBC)), const((1, SSM_XBC)),
                const((8, LANES)), const((1, SSM_W)), const((1, SSM_W)), st_spec]
    kern, inputs, in_specs, aliases = _in_place_state(
        _ssd_sample_kernel, [proj, buf_t, conv_w, conv_b, pc, d_row, norm_w, state_all], in_specs, prev_out, 1)
    return pl.pallas_call(
        kern,
        grid=(n_tok // TOK_GROUP,),
        in_specs=in_specs,
        out_specs=[const((n_tok, SSM_W)), st_spec, const((3, n_tok, SSM_XBC))],
        out_shape=[_sds((n_tok, SSM_W), BF16), _sds(state_all.shape, F32), _sds((3, n_tok, SSM_XBC), F32)],
        input_output_aliases=aliases,
        scratch_shapes=[pltpu.VMEM((SSM_W, 2 * n_tok), BF16),
                        pltpu.VMEM((n_tok, SSM_W), F32),
                        pltpu.VMEM((n_tok, SSM_BC), F32),
                        pltpu.VMEM((n_tok, SSM_BC), F32),
                        pltpu.VMEM((SSM_HEADS, n_tok, LANES), F32),
                        pltpu.VMEM((SSM_W, n_tok), F32)],
        compiler_params=_params(1),
        name="ssd_sample",
    )(*inputs)


def _cfm_sample_kernel(p_ref, buf_ref, cw_ref, cb_ref, lnw_ref, lnb_ref, o_ref, bo_ref):
    glu = p_ref[:, CFM_COL_GLU:CFM_COL_GLU + 2 * CFM_W]
    u0 = glu[:, :CFM_W] * _sigmoid(glu[:, CFM_W:])
    acc = cw_ref[CFM_K - 1:CFM_K, :] * u0 + cb_ref[...]
    for j in range(CFM_K - 1):
        row = buf_ref[j]
        acc = acc + cw_ref[j:j + 1, :] * row
        if j > 0:
            bo_ref[j - 1] = row
    bo_ref[CFM_K - 2] = u0
    gate = p_ref[:, CFM_COL_GC:CFM_COL_GC + CFM_W]
    o_ref[...] = _layernorm_silu_gate(acc, gate, lnw_ref[...], lnb_ref[...]).astype(BF16)


def _cfm_sample(proj, buf_t, conv_w, conv_b, ln_w, ln_b):
    n_tok = proj.shape[0]
    const = lambda shape: pl.BlockSpec(shape, lambda i: (0,) * len(shape))
    return pl.pallas_call(
        _cfm_sample_kernel,
        grid=(1,),
        in_specs=[const((n_tok, CFM_PROJ_W)), const((CFM_K - 1, n_tok, CFM_W)), const((CFM_K, CFM_W)),
                  const((1, CFM_W)), const((1, CFM_W)), const((1, CFM_W))],
        out_specs=[const((n_tok, CFM_W)), const((CFM_K - 1, n_tok, CFM_W))],
        out_shape=[_sds((n_tok, CFM_W), BF16), _sds((CFM_K - 1, n_tok, CFM_W), F32)],
        compiler_params=_params(1),
        name="cfm_sample",
    )(proj, buf_t, conv_w, conv_b, ln_w, ln_b)


_IN_Q, _IN_ZA, _IN_BA, _IN_AA, _IN_ZB, _IN_XBC, _IN_DT, _IN_GLU, _IN_GC, _IN_GATES, _IN_END = (
    0, 1536, 2048, 2052, 2056, 3080, 4616, 4632, 5656, 6168, 9240)


def _prep_w_in(w_in):
    seg = lambda a, b: w_in[:, :, a:b]
    zeros = lambda n: jnp.zeros(w_in.shape[:2] + (n,), w_in.dtype)
    w_gdn = jnp.concatenate([seg(_IN_Q, _IN_ZA), seg(_IN_ZA, _IN_BA),
                             seg(_IN_BA, _IN_AA), zeros(LANES - GDN_HEADS),
                             seg(_IN_AA, _IN_ZB), zeros(LANES - GDN_HEADS)], axis=-1)
    w_ssd = jnp.concatenate([seg(_IN_XBC, _IN_DT), seg(_IN_ZB, _IN_XBC),
                             seg(_IN_DT, _IN_GLU), zeros(LANES - SSM_HEADS)], axis=-1)
    w_cfm = seg(_IN_GLU, _IN_GATES)
    w_gate = seg(_IN_GATES, _IN_END)
    return tuple(w.astype(BF16) for w in (w_gdn, w_ssd, w_cfm, w_gate))


def _lane_pad(v, rows=8):
    out = jnp.zeros((v[0].shape[0], rows, LANES), F32)
    for i, a in enumerate(v):
        out = out.at[:, i, :a.shape[1]].set(a)
    return out


def kernel(x_prompt, x_sample, state_gdn, state_gdn_conv, state_ssm, state_ssm_conv, state_cfm_conv,
           norm_w, w_in, gdn_conv_w, gdn_a_log, gdn_dt_bias, gdn_norm_w, gdn_w_o,
           ssm_conv_w, ssm_conv_b, ssm_a_log, ssm_dt_bias, ssm_d, ssm_norm_w, ssm_w_o,
           cfm_conv_w, cfm_conv_b, cfm_ln_w, cfm_ln_b, cfm_w_o, w_out, final_norm_w):
    depth = w_in.shape[0]
    batch, seq, _ = x_prompt.shape
    n_tok = x_sample.shape[0]
    assert n_tok == LANES and seq % PROMPT_T == 0

    w_gdn, w_ssd, w_cfm, w_gate = _prep_w_in(w_in)
    wa, wb, wc, wo = (w.astype(BF16) for w in (gdn_w_o, ssm_w_o, cfm_w_o, w_out))
    gdn_pc = _lane_pad([gdn_a_log, gdn_dt_bias])
    ssm_pc = _lane_pad([ssm_a_log, ssm_dt_bias])
    ssm_d_row = jnp.repeat(ssm_d, SSM_P, axis=1)[:, None, :]
    final_w = final_norm_w[None, :]

    hp = x_prompt.reshape(batch * seq, D_MODEL)
    hs = x_sample.reshape(n_tok, D_MODEL)
    p_states, s_states = [], []
    new_gdn_s = new_ssm_s = None
    for i in range(depth):
        final = i == depth - 1
        row = lambda a: a[i][None, :]
        nw = row(norm_w)

        gdn_p = (gdn_conv_w[i], gdn_pc[i], row(gdn_norm_w))
        ssd_p = (ssm_conv_w[i], row(ssm_conv_b), ssm_pc[i], ssm_d_row[i], row(ssm_norm_w))
        cfm_p = (cfm_conv_w[i], row(cfm_conv_b), row(cfm_ln_w), row(cfm_ln_b))
        hp, gdn_s, gdn_cv, ssm_s, ssm_cv, cfm_cv = _prompt_layer(
            hp, nw, (w_gdn, w_ssd, w_cfm, w_gate, wa, wb, wc, wo), i, final_w, gdn_p, ssd_p, cfm_p,
            batch, seq, final)
        p_states.append((gdn_s, gdn_cv[:, CONV_HIST - 3:], ssm_s, ssm_cv[:, CONV_HIST - 3:],
                         cfm_cv[:, CFM_HIST - (CFM_K - 1):]))

        oa, new_gdn_s, gdn_buf = _gdn_sample(_sample_proj(hs, nw, w_gdn, i), jnp.swapaxes(state_gdn_conv[i], 0, 1),
                                             gdn_conv_w[i], gdn_pc[i], row(gdn_norm_w), state_gdn, i, new_gdn_s)
        yb, new_ssm_s, ssm_buf = _ssd_sample(_sample_proj(hs, nw, w_ssd, i), jnp.swapaxes(state_ssm_conv[i], 0, 1),
                                             ssm_conv_w[i], row(ssm_conv_b), ssm_pc[i], ssm_d_row[i],
                                             row(ssm_norm_w), state_ssm, i, new_ssm_s)
        uc, cfm_buf = _cfm_sample(_sample_proj(hs, nw, w_cfm, i), jnp.swapaxes(state_cfm_conv[i], 0, 1),
                                  cfm_conv_w[i], row(cfm_conv_b), row(cfm_ln_w), row(cfm_ln_b))
        hs = _out_proj(hs, nw, w_gate, i, oa, yb, uc, wa, wb, wc, wo, final_w, final)
        s_states.append((jnp.swapaxes(gdn_buf, 0, 1), jnp.swapaxes(ssm_buf, 0, 1), jnp.swapaxes(cfm_buf, 0, 1)))

    stack = lambda states, j: jnp.stack([s[j] for s in states])
    return (hp.reshape(batch, seq, D_MODEL), hs.reshape(n_tok, 1, D_MODEL),
            stack(p_states, 0), stack(p_states, 1), stack(p_states, 2), stack(p_states, 3), stack(p_states, 4),
            new_gdn_s, stack(s_states, 0), new_ssm_s, stack(s_states, 1), stack(s_states, 2))
```

```python
import functools

import jax
import jax.numpy as jnp
from jax import lax
from jax.experimental import pallas as pl
from jax.experimental.pallas import tpu as pltpu

F32 = jnp.float32
BF16 = jnp.bfloat16

D_MODEL = 1024
GDN_HEADS = 4
GDN_D = 128
GDN_QK_W = GDN_HEADS * GDN_D
GDN_CH = 3 * GDN_QK_W
SSM_W = 1024
SSM_P = 64
SSM_HEADS = 16
SSM_GROUPS = 2
SSM_RPG = SSM_HEADS // SSM_GROUPS
SSM_GW = SSM_RPG * SSM_P
SSM_N = 128
SSM_BC = SSM_GROUPS * SSM_N
SSM_XBC = SSM_W + 2 * SSM_BC
CFM_W = 512
CFM_K = 31
CFM_HIST = 32
CFM_ROWS = 32
CONV_HIST = 8
CHUNK = 64
CHUNK_SHIFT = 6
EPS = 1e-6
NEG = -1e30
LANES = 128

GDN_COL_QKV, GDN_COL_ZA, GDN_COL_BA, GDN_COL_AA, GDN_PROJ_W = 0, 1536, 2048, 2176, 2304
SSD_COL_XBC, SSD_COL_ZB, SSD_COL_DT, SSD_PROJ_W = 0, 1536, 2560, 2688
CFM_COL_GLU, CFM_COL_GC, CFM_PROJ_W = 0, 1024, 1536
GATE_PROJ_W = 3 * D_MODEL

PROMPT_T = 256
VMEM_LIMIT = 56 * 1024 * 1024


def _sds(shape, dtype):
    return jax.ShapeDtypeStruct(shape, dtype)


def _params(n_axes):
    return pltpu.CompilerParams(dimension_semantics=("arbitrary",) * n_axes,
                                vmem_limit_bytes=VMEM_LIMIT)


def _sigmoid(x):
    return jax.nn.sigmoid(x)


def _silu(x):
    return x * jax.nn.sigmoid(x)


def _softplus(x):
    return jnp.maximum(x, 0.0) + jnp.log1p(jnp.exp(-jnp.abs(x)))


def _mm(a, b):
    return jnp.dot(a.astype(BF16), b.astype(BF16), preferred_element_type=F32)


def _mm_nt(a, b):
    return lax.dot_general(a.astype(BF16), b.astype(BF16), (((1,), (1,)), ((), ())),
                           preferred_element_type=F32)


def _mm_tn(a, b):
    return lax.dot_general(a.astype(BF16), b.astype(BF16), (((0,), (0,)), ((), ())),
                           preferred_element_type=F32)


def _rms(x, w):
    return x * lax.rsqrt(jnp.mean(x * x, axis=-1, keepdims=True) + EPS) * w


def _norm_proj(x_ref, nw_ref, w_ref):
    h = _rms(x_ref[...], nw_ref[...]).astype(BF16)
    return jnp.dot(h, w_ref[...], preferred_element_type=F32)


def _iota(shape, dim):
    return lax.broadcasted_iota(jnp.int32, shape, dim)


def _chunk_cumsum(x):
    pos = _iota(x.shape, 0) & (CHUNK - 1)
    d = 1
    while d < CHUNK:
        x = x + jnp.where(pos >= d, pltpu.roll(x, d, axis=0), 0.0)
        d *= 2
    return x


def _chunk_last(x):
    n = x.shape[0] // CHUNK
    return jnp.concatenate(
        [jnp.broadcast_to(x[(c + 1) * CHUNK - 1:(c + 1) * CHUNK, :], (CHUNK, x.shape[1])) for c in range(n)],
        axis=0)


def _expand_heads(x, emat):
    hi = x.astype(BF16).astype(F32)
    r1 = x - hi
    mid = r1.astype(BF16).astype(F32)
    lo = r1 - mid
    packed = hi + pltpu.roll(mid, SSM_HEADS, axis=1) + pltpu.roll(lo, 2 * SSM_HEADS, axis=1)
    return jnp.dot(packed.astype(BF16), emat, preferred_element_type=F32)


def _expand_matrix():
    r = _iota((LANES, SSM_W), 0)
    c = _iota((LANES, SSM_W), 1)
    return jnp.where(((r & (SSM_HEADS - 1)) == (c >> CHUNK_SHIFT)) & (r < 3 * SSM_HEADS), 1.0, 0.0).astype(BF16)


def _causal_conv4(x, xe_scr, w_ref):
    t_len = x.shape[0]
    xe = jnp.concatenate([xe_scr[0:CONV_HIST, :], x], axis=0)
    acc = xe * w_ref[0:1, :]
    for j in range(1, 4):
        acc = pltpu.roll(acc, 1, axis=0) + xe * w_ref[j:j + 1, :]
    new_hist = x[t_len - CONV_HIST:, :]
    xe_scr[0:CONV_HIST, :] = new_hist
    return acc[CONV_HIST:, :], new_hist


GDN_GROUP = PROMPT_T // CHUNK
GDN_CAT = GDN_GROUP * CHUNK
GDN_D_SHIFT = 7


def _lane_cat(x):
    return jnp.concatenate([x[c * CHUNK:(c + 1) * CHUNK, :] for c in range(GDN_GROUP)], axis=1)


def _gdn_pre(proj, cw_ref, pc_ref, cv_ref, xe_scr):
    y, new_hist = _causal_conv4(proj[:, GDN_COL_QKV:GDN_COL_QKV + GDN_CH], xe_scr, cw_ref)
    cv_ref[0] = new_hist
    qkv = _silu(y)
    q_n, k_n, v_n = [], [], []
    for h in range(GDN_HEADS):
        q = qkv[:, h * GDN_D:(h + 1) * GDN_D]
        k = qkv[:, GDN_QK_W + h * GDN_D:GDN_QK_W + (h + 1) * GDN_D]
        q_n.append(q * lax.rsqrt(jnp.sum(q * q, axis=-1, keepdims=True) + EPS) * (GDN_D ** -0.5))
        k_n.append(k * lax.rsqrt(jnp.sum(k * k, axis=-1, keepdims=True) + EPS))
        v_n.append(qkv[:, 2 * GDN_QK_W + h * GDN_D:2 * GDN_QK_W + (h + 1) * GDN_D])
    beta_c = _sigmoid(proj[:, GDN_COL_BA:GDN_COL_BA + LANES])
    g_c = -jnp.exp(pc_ref[0:1, :]) * _softplus(proj[:, GDN_COL_AA:GDN_COL_AA + LANES] + pc_ref[1:2, :])
    return q_n, k_n, v_n, beta_c, g_c


def _gdn_main(pre, za, nw_ref, bd_ref, km_ref, st_ref, s_scr, o_scr):
    q_n, k_n, v_n, beta_c, g_c = pre
    heads = range(GDN_HEADS)
    gam_c = _chunk_cumsum(g_c)
    glast_c = _chunk_last(gam_c)
    eg_c = jnp.exp(gam_c)
    ekd_c = jnp.exp(glast_c - gam_c)
    dtot_c = jnp.exp(glast_c)
    bg_c = beta_c * eg_c
    gam_r = gam_c.T
    col = lambda a, h: a[:, h:h + 1]

    t_i = _iota((CHUNK, GDN_CAT), 0)
    s_i = _iota((CHUNK, GDN_CAT), 1) & (CHUNK - 1)
    incl, strict = t_i >= s_i, t_i > s_i
    eye_cat = jnp.where(t_i == s_i, 1.0, 0.0)
    bd_mask = bd_ref[...]
    k_mask = km_ref[...]
    low_half = _iota((1, LANES), 1) < CHUNK

    def block_diag(y_cat):
        return jnp.concatenate([y_cat.astype(BF16)] * GDN_GROUP, axis=0) * bd_mask

    kb = [k_n[h] * col(beta_c, h) for h in heads]
    aq = []
    for h in heads:
        rhs_nt = jnp.concatenate([k_n[h].astype(BF16)] * GDN_GROUP, axis=1) * k_mask
        aq.append(_mm_nt(jnp.concatenate([_lane_cat(kb[h]), _lane_cat(q_n[h])], axis=0), rhs_nt))

    dincl = []
    for h in heads:
        gcol = col(gam_c, h)
        tiles = []
        for m in range(GDN_GROUP // 2):
            lo = jnp.broadcast_to(gcol[2 * m * CHUNK:(2 * m + 1) * CHUNK, :], (CHUNK, LANES))
            hi = jnp.broadcast_to(gcol[(2 * m + 1) * CHUNK:(2 * m + 2) * CHUNK, :], (CHUNK, LANES))
            tiles.append(jnp.where(low_half, lo, hi))
        dincl.append(jnp.exp(jnp.where(incl, jnp.concatenate(tiles, axis=1) - gam_r[h:h + 1, :], NEG)))
    n0 = [aq[h][:CHUNK, :] * jnp.where(strict, dincl[h], 0.0) for h in heads]
    attn_bd = [block_diag(aq[h][CHUNK:, :] * dincl[h]) for h in heads]

    p = [eye_cat - n0[h] for h in heads]
    pw = [_mm(n0[h], block_diag(n0[h])) for h in heads]
    for _ in range(4):
        both = [_mm(jnp.concatenate([pw[h], p[h]], axis=0), block_diag(pw[h])) for h in heads]
        pw = [both[h][:CHUNK, :] for h in heads]
        p = [p[h] + both[h][CHUNK:, :] for h in heads]
    tinv = [p[h] + _mm(p[h], block_diag(pw[h])) for h in heads]

    wu = [_mm(block_diag(tinv[h]),
              jnp.concatenate([k_n[h] * col(bg_c, h), v_n[h] * col(beta_c, h)], axis=1)) for h in heads]
    qg = [q_n[h] * col(eg_c, h) for h in heads]
    kd = [k_n[h] * col(ekd_c, h) for h in heads]

    state = [s_scr[h] for h in heads]
    zeros = jnp.zeros((CHUNK, GDN_D), F32)
    for c in range(GDN_GROUP):
        rows = slice(c * CHUNK, (c + 1) * CHUNK)
        ws_qs = [_mm(jnp.concatenate([wu[h][rows, :GDN_D], qg[h][rows, :]], axis=0), state[h]) for h in heads]
        u = [wu[h][rows, GDN_D:] - ws_qs[h][:CHUNK, :] for h in heads]
        for h in heads:
            u_pad = jnp.concatenate([zeros] * c + [u[h]] + [zeros] * (GDN_GROUP - 1 - c), axis=0)
            o_scr[rows, h * GDN_D:(h + 1) * GDN_D] = ws_qs[h][CHUNK:, :] + _mm(attn_bd[h][rows, :], u_pad)
        state = [state[h] * dtot_c[c * CHUNK:c * CHUNK + 1, h:h + 1] + _mm_tn(kd[h][rows, :], u[h])
                 for h in heads]
    for h in heads:
        s_scr[h] = state[h]

    o = o_scr[...]
    outs = []
    for h in heads:
        sl = slice(h * GDN_D, (h + 1) * GDN_D)
        outs.append(_rms(o[:, sl], nw_ref[...]) * _silu(za[:, sl]))
    st_ref[0] = s_scr[...]
    return jnp.concatenate(outs, axis=1).astype(BF16)


def _ssd_pre(proj, cw_ref, cb_ref, pc_ref, cv_ref, xe_scr):
    y, new_hist = _causal_conv4(proj[:, SSD_COL_XBC:SSD_COL_XBC + SSM_XBC], xe_scr, cw_ref)
    cv_ref[0] = new_hist
    xbc = _silu(y + cb_ref[...])
    head_lane = _iota((1, LANES), 1) < SSM_HEADS
    dt_c = jnp.where(head_lane, _softplus(proj[:, SSD_COL_DT:SSD_COL_DT + LANES] + pc_ref[1:2, :]), 0.0)
    return xbc, dt_c


def _ssd_main(pre, zb, pc_ref, d_ref, nw_ref, emat_ref, st_ref, st_scr, y_scr):
    xbc, dt_c = pre
    xs = xbc[:, :SSM_W]
    bm = xbc[:, SSM_W:SSM_W + SSM_BC]
    cm = xbc[:, SSM_W + SSM_BC:]
    head_lane = _iota((1, LANES), 1) < SSM_HEADS
    gam_c = _chunk_cumsum(dt_c * (-jnp.exp(pc_ref[0:1, :])))
    glast_c = _chunk_last(gam_c)
    emat = emat_ref[...]
    e1 = _expand_heads(jnp.where(head_lane, jnp.exp(gam_c), 0.0), emat)
    e2 = _expand_heads(jnp.exp(glast_c - gam_c) * dt_c, emat)
    gam_r = gam_c.T
    dt_r = dt_c.T

    t_i = _iota((CHUNK, CHUNK), 0)
    s_i = _iota((CHUNK, CHUNK), 1)
    incl = t_i >= s_i
    low_half = _iota((1, LANES), 1) < SSM_P

    for c in range(PROMPT_T // CHUNK):
        rows = slice(c * CHUNK, (c + 1) * CHUNK)
        xs_c = xs[rows, :]
        bm_c = bm[rows, :]
        cm_c = cm[rows, :]
        gam_cc = gam_c[rows, :]
        cbs = [_mm_nt(cm_c[:, g * SSM_N:(g + 1) * SSM_N], bm_c[:, g * SSM_N:(g + 1) * SSM_N])
               for g in range(SSM_GROUPS)]
        pairs = []
        for j in range(SSM_HEADS // 2):
            xp = xs_c[:, j * LANES:(j + 1) * LANES]
            acc = None
            for half in range(2):
                h = 2 * j + half
                m = (cbs[h // SSM_RPG]
                     * jnp.exp(jnp.where(incl, gam_cc[:, h:h + 1] - gam_r[h:h + 1, rows], NEG))
                     * dt_r[h:h + 1, rows])
                xm = jnp.where(low_half if half == 0 else jnp.logical_not(low_half), xp, 0.0)
                part = _mm(m, xm)
                acc = part if acc is None else acc + part
            pairs.append(acc)
        y_intra = jnp.concatenate(pairs, axis=1)
        y_inter = jnp.concatenate(
            [_mm(cm_c[:, g * SSM_N:(g + 1) * SSM_N], st_scr[g]) for g in range(SSM_GROUPS)], axis=1)
        y_scr[rows, :] = y_intra + y_inter * e1[rows, :] + d_ref[...] * xs_c
        xe = xs_c * e2[rows, :]
        dtot = e1[(c + 1) * CHUNK - 1:(c + 1) * CHUNK, :]
        for g in range(SSM_GROUPS):
            sl = slice(g * SSM_GW, (g + 1) * SSM_GW)
            st_scr[g] = st_scr[g] * dtot[:, sl] + _mm_tn(bm_c[:, g * SSM_N:(g + 1) * SSM_N], xe[:, sl])

    @pl.when(pl.program_id(1) == pl.num_programs(1) - 1)
    def _():
        for g in range(SSM_GROUPS):
            st_ref[0, g * SSM_RPG:(g + 1) * SSM_RPG] = st_scr[g].T.reshape(SSM_RPG, SSM_P, SSM_N)

    return _rms(y_scr[...] * _silu(zb), nw_ref[...]).astype(BF16)


def _layernorm_silu_gate(u, gate, lnw, lnb):
    mu = jnp.mean(u, axis=-1, keepdims=True)
    uc = u - mu
    var = jnp.mean(uc * uc, axis=-1, keepdims=True)
    return _silu(uc * lax.rsqrt(var + EPS) * lnw + lnb) * _silu(gate)


def _cfm_stage(proj, cw_ref, cb_ref, lnw_ref, lnb_ref, cv_ref, hist_scr):
    glu = proj[:, CFM_COL_GLU:CFM_COL_GLU + 2 * CFM_W]
    u0 = glu[:, :CFM_W] * _sigmoid(glu[:, CFM_W:])
    xe = jnp.concatenate([hist_scr[...], u0], axis=0)
    rolled = [xe] + [pltpu.roll(xe, b, axis=0) for b in range(1, 8)]
    new_hist = u0[PROMPT_T - CFM_HIST:, :]
    hist_scr[...] = new_hist
    cv_ref[0] = new_hist
    gate = proj[:, CFM_COL_GC:CFM_COL_GC + CFM_W]
    outs = []
    for r0 in range(0, PROMPT_T, CFM_ROWS):
        acc = u0[r0:r0 + CFM_ROWS, :] * cw_ref[CFM_K - 1:CFM_K, :]
        for s in range(1, CFM_K):
            a, b = divmod(s, 8)
            lo = CFM_HIST - 8 * a + r0
            acc = acc + rolled[b][lo:lo + CFM_ROWS, :] * cw_ref[CFM_K - 1 - s:CFM_K - s, :]
        outs.append(_layernorm_silu_gate(acc + cb_ref[...], gate[r0:r0 + CFM_ROWS, :], lnw_ref[...],
                                         lnb_ref[...]).astype(BF16))
    return jnp.concatenate(outs, axis=0)


def _merge_stage(x, gates, oa, yb, uc, wa_ref, wb_ref, wc_ref, wo_ref, fw_ref, final):
    dot = functools.partial(jnp.dot, preferred_element_type=F32)
    merged = (_sigmoid(gates[:, :D_MODEL]) * dot(oa, wa_ref[...])
              + _sigmoid(gates[:, D_MODEL:2 * D_MODEL]) * dot(yb, wb_ref[...])
              + _sigmoid(gates[:, 2 * D_MODEL:]) * dot(uc, wc_ref[...]))
    out = x + dot(merged.astype(BF16), wo_ref[...])
    return _rms(out, fw_ref[...]) if final else out


def _prompt_layer_kernel(x_ref, nwin_ref, wg_ref, ws_ref, wc_ref, wgate_ref, wa_ref, wb_ref, wcc_ref, wo_ref, fw_ref,
                         g_cw, g_pc, g_nw, s_cw, s_cb, s_pc, s_d, s_nw, c_cw, c_cb, c_lnw, c_lnb,
                         bd_ref, km_ref, emat_ref,
                         o_ref, gst_ref, gcv_ref, sst_ref, scv_ref, ccv_ref,
                         s_scr, gxe_scr, o_scr, st_scr, sxe_scr, y_scr, chist_scr, *, final):
    @pl.when(pl.program_id(1) == 0)
    def _():
        s_scr[...] = jnp.zeros_like(s_scr)
        st_scr[...] = jnp.zeros_like(st_scr)
        chist_scr[...] = jnp.zeros_like(chist_scr)
        gxe_scr[0:CONV_HIST, :] = jnp.zeros((CONV_HIST, GDN_CH), F32)
        sxe_scr[0:CONV_HIST, :] = jnp.zeros((CONV_HIST, SSM_XBC), F32)

    dot = functools.partial(jnp.dot, preferred_element_type=F32)
    x = x_ref[...]
    h = _rms(x, nwin_ref[...]).astype(BF16)
    pc = dot(h, wc_ref[...])
    pg = dot(h, wg_ref[...])
    uc = _cfm_stage(pc, c_cw, c_cb, c_lnw, c_lnb, ccv_ref, chist_scr)
    ps = dot(h, ws_ref[...])
    gdn_pre = _gdn_pre(pg, g_cw, g_pc, gcv_ref, gxe_scr)
    pgate = dot(h, wgate_ref[...])
    ssd_pre = _ssd_pre(ps, s_cw, s_cb, s_pc, scv_ref, sxe_scr)
    oa = _gdn_main(gdn_pre, pg[:, GDN_COL_ZA:GDN_COL_ZA + GDN_QK_W], g_nw, bd_ref, km_ref, gst_ref, s_scr, o_scr)
    yb = _ssd_main(ssd_pre, ps[:, SSD_COL_ZB:SSD_COL_ZB + SSM_W], s_pc, s_d, s_nw, emat_ref, sst_ref, st_scr,
                   y_scr)
    o_ref[...] = _merge_stage(x, pgate, oa, yb, uc, wa_ref, wb_ref, wcc_ref, wo_ref, fw_ref, final)


def _prompt_layer(x, norm_w, weights, layer, final_w, gdn_p, ssd_p, cfm_p, batch, seq, final):
    nt = seq // PROMPT_T
    tok = lambda width: pl.BlockSpec((PROMPT_T, width), lambda b, t: (b * nt + t, 0))
    const = lambda a: pl.BlockSpec(a.shape, lambda b, t: (0,) * a.ndim)
    resident = lambda a: pl.BlockSpec((None,) + a.shape[1:], lambda b, t: (layer,) + (0,) * (a.ndim - 1),
                                      pipeline_mode=pl.Buffered(1))
    per_seq = lambda *shape: pl.BlockSpec((1,) + shape, lambda b, t: (b,) + (0,) * len(shape))
    chunk_of = jnp.arange(GDN_CAT) // CHUNK
    bd_mask = (chunk_of[:, None] == chunk_of[None, :]).astype(BF16)
    k_mask = (chunk_of[:, None] == (jnp.arange(GDN_GROUP * GDN_D) // GDN_D)[None, :]).astype(BF16)
    small = [final_w] + list(gdn_p) + list(ssd_p) + list(cfm_p) + [bd_mask, k_mask, _expand_matrix()]
    inputs = [x, norm_w] + list(weights) + small
    in_specs = ([tok(D_MODEL), const(norm_w)] + [resident(w) for w in weights] + [const(a) for a in small])
    return pl.pallas_call(
        functools.partial(_prompt_layer_kernel, final=final),
        grid=(batch, nt),
        in_specs=in_specs,
        out_specs=[tok(D_MODEL),
                   per_seq(GDN_HEADS, GDN_D, GDN_D), per_seq(CONV_HIST, GDN_CH),
                   per_seq(SSM_HEADS, SSM_P, SSM_N), per_seq(CONV_HIST, SSM_XBC),
                   per_seq(CFM_HIST, CFM_W)],
        out_shape=[_sds((batch * seq, D_MODEL), F32),
                   _sds((batch, GDN_HEADS, GDN_D, GDN_D), F32), _sds((batch, CONV_HIST, GDN_CH), F32),
                   _sds((batch, SSM_HEADS, SSM_P, SSM_N), F32), _sds((batch, CONV_HIST, SSM_XBC), F32),
                   _sds((batch, CFM_HIST, CFM_W), F32)],
        scratch_shapes=[pltpu.VMEM((GDN_HEADS, GDN_D, GDN_D), F32),
                        pltpu.VMEM((CONV_HIST + PROMPT_T, GDN_CH), F32),
                        pltpu.VMEM((PROMPT_T, GDN_QK_W), F32),
                        pltpu.VMEM((SSM_GROUPS, SSM_N, SSM_GW), F32),
                        pltpu.VMEM((CONV_HIST + PROMPT_T, SSM_XBC), F32),
                        pltpu.VMEM((PROMPT_T, SSM_W), F32),
                        pltpu.VMEM((CFM_HIST, CFM_W), F32)],
        compiler_params=_params(2),
        name="prompt_layer",
    )(*inputs)


def _out_kernel(x_ref, nwin_ref, wg_ref, oa_ref, yb_ref, uc_ref, wa_ref, wb_ref, wc_ref, wo_ref, fw_ref,
                o_ref, *, final):
    x = x_ref[...]
    gates = jnp.dot(_rms(x, nwin_ref[...]).astype(BF16), wg_ref[...], preferred_element_type=F32)
    o_ref[...] = _merge_stage(x, gates, oa_ref[...], yb_ref[...], uc_ref[...],
                              wa_ref, wb_ref, wc_ref, wo_ref, fw_ref, final)


def _out_proj(x, norm_w, w_gate, layer, oa, yb, uc, wa, wb, wc, wo, final_w, final):
    m = x.shape[0]
    const = lambda shape: pl.BlockSpec(shape, lambda i: (0,) * len(shape))
    lw = lambda shape: pl.BlockSpec((None,) + shape, lambda i: (layer,) + (0,) * len(shape))
    return pl.pallas_call(
        functools.partial(_out_kernel, final=final),
        grid=(1,),
        in_specs=[const((m, D_MODEL)), const((1, D_MODEL)), lw((D_MODEL, GATE_PROJ_W)),
                  const((m, GDN_QK_W)), const((m, SSM_W)), const((m, CFM_W)),
                  lw((GDN_QK_W, D_MODEL)), lw((SSM_W, D_MODEL)), lw((CFM_W, D_MODEL)),
                  lw((D_MODEL, D_MODEL)), const((1, D_MODEL))],
        out_specs=const((m, D_MODEL)),
        out_shape=_sds((m, D_MODEL), F32),
        compiler_params=_params(1),
        name="merge_out",
    )(x, norm_w, w_gate, oa, yb, uc, wa, wb, wc, wo, final_w)


SAMPLE_TN = 384


def _proj_kernel(x_ref, nwin_ref, w_ref, o_ref):
    o_ref[...] = _norm_proj(x_ref, nwin_ref, w_ref)


def _sample_proj(x, norm_w, w, layer):
    m = x.shape[0]
    width = w.shape[-1]
    return pl.pallas_call(
        _proj_kernel,
        grid=(width // SAMPLE_TN,),
        in_specs=[pl.BlockSpec((m, D_MODEL), lambda j: (0, 0)),
                  pl.BlockSpec((1, D_MODEL), lambda j: (0, 0)),
                  pl.BlockSpec((None, D_MODEL, SAMPLE_TN), lambda j: (layer, 0, j))],
        out_specs=pl.BlockSpec((m, SAMPLE_TN), lambda j: (0, j)),
        out_shape=_sds((m, width), F32),
        compiler_params=_params(1),
        name="sample_proj",
    )(x, norm_w, w)


TOK_GROUP = 8


def _gdn_sample_kernel(p_ref, buf_ref, cw_ref, pc_ref, nw_ref, s_ref,
                       o_ref, so_ref, bo_ref,
                       kt_scr, qt_scr, v_scr, a_scr, b_scr, qk_scr, o_scr):
    tg = pl.program_id(0)

    @pl.when(tg == 0)
    def _():
        x = p_ref[:, GDN_COL_QKV:GDN_COL_QKV + GDN_CH]
        y = (cw_ref[0:1, :] * buf_ref[0] + cw_ref[1:2, :] * buf_ref[1]
             + cw_ref[2:3, :] * buf_ref[2] + cw_ref[3:4, :] * x)
        bo_ref[0] = buf_ref[1]
        bo_ref[1] = buf_ref[2]
        bo_ref[2] = x
        qkv = _silu(y)
        beta = _sigmoid(p_ref[:, GDN_COL_BA:GDN_COL_BA + LANES])
        decay = jnp.exp(-jnp.exp(pc_ref[0:1, :])
                        * _softplus(p_ref[:, GDN_COL_AA:GDN_COL_AA + LANES] + pc_ref[1:2, :]))
        n_tok = x.shape[0]
        for h in range(GDN_HEADS):
            q = qkv[:, h * GDN_D:(h + 1) * GDN_D]
            k = qkv[:, GDN_QK_W + h * GDN_D:GDN_QK_W + (h + 1) * GDN_D]
            q = q * lax.rsqrt(jnp.sum(q * q, axis=-1, keepdims=True) + EPS) * (GDN_D ** -0.5)
            k = k * lax.rsqrt(jnp.sum(k * k, axis=-1, keepdims=True) + EPS)
            kt_scr[h] = k.T
            qt_scr[h] = q.T
            a_scr[h] = jnp.broadcast_to(decay[:, h:h + 1], (n_tok, LANES))
            b_scr[h] = jnp.broadcast_to(beta[:, h:h + 1], (n_tok, LANES))
            qk_scr[h] = jnp.broadcast_to(jnp.sum(q * k, axis=-1, keepdims=True), (n_tok, LANES))
        v_scr[...] = qkv[:, 2 * GDN_QK_W:]

    shift = (LANES - TOK_GROUP * tg) & (LANES - 1)
    rows = pl.ds(pl.multiple_of(tg * TOK_GROUP, TOK_GROUP), TOK_GROUP)
    for h in range(GDN_HEADS):
        ktg = pltpu.roll(kt_scr[h], shift, axis=1)
        qtg = pltpu.roll(qt_scr[h], shift, axis=1)
        a_blk = a_scr[h, rows, :]
        b_blk = b_scr[h, rows, :]
        qk_blk = qk_scr[h, rows, :]
        v_blk = v_scr[rows, h * GDN_D:(h + 1) * GDN_D]
        o_rows = []
        for j in range(TOK_GROUP):
            s = s_ref[j, h]
            kcol = ktg[:, j:j + 1]
            qcol = qtg[:, j:j + 1]
            ks = jnp.sum(s * kcol, axis=0, keepdims=True)
            qs = jnp.sum(s * qcol, axis=0, keepdims=True)
            a_row = a_blk[j:j + 1, :]
            delta = b_blk[j:j + 1, :] * (v_blk[j:j + 1, :] - a_row * ks)
            so_ref[j, h] = a_row * s + kcol * delta
            o_rows.append(a_row * qs + qk_blk[j:j + 1, :] * delta)
        o_scr[rows, h * GDN_D:(h + 1) * GDN_D] = jnp.concatenate(o_rows, axis=0)

    @pl.when(tg == pl.num_programs(0) - 1)
    def _():
        o = o_scr[...]
        za = p_ref[:, GDN_COL_ZA:GDN_COL_ZA + GDN_QK_W]
        outs = []
        for h in range(GDN_HEADS):
            sl = slice(h * GDN_D, (h + 1) * GDN_D)
            outs.append(_rms(o[:, sl], nw_ref[...]) * _silu(za[:, sl]))
        o_ref[...] = jnp.concatenate(outs, axis=1).astype(BF16)


def _in_place_state(kernel_fn, inputs, in_specs, prev_out, out_index):
    if prev_out is None:
        return kernel_fn, inputs, in_specs, {}
    n = len(inputs)
    wrapped = lambda *refs: kernel_fn(*refs[:n], *refs[n + 1:])
    return wrapped, inputs + [prev_out], in_specs + [pl.BlockSpec(memory_space=pl.ANY)], {n: out_index}


def _gdn_sample(proj, buf_t, conv_w, pc, norm_w, state_all, layer, prev_out):
    n_tok = proj.shape[0]
    const = lambda shape: pl.BlockSpec(shape, lambda g: (0,) * len(shape))
    st_spec = pl.BlockSpec((None, TOK_GROUP, GDN_HEADS, GDN_D, GDN_D), lambda g: (layer, g, 0, 0, 0))
    in_specs = [const((n_tok, GDN_PROJ_W)), const((3, n_tok, GDN_CH)), const((4, GDN_CH)),
                const((8, LANES)), const((1, GDN_D)), st_spec]
    kern, inputs, in_specs, aliases = _in_place_state(
        _gdn_sample_kernel, [proj, buf_t, conv_w, pc, norm_w, state_all], in_specs, prev_out, 1)
    return pl.pallas_call(
        kern,
        grid=(n_tok // TOK_GROUP,),
        in_specs=in_specs,
        out_specs=[const((n_tok, GDN_QK_W)), st_spec, const((3, n_tok, GDN_CH))],
        out_shape=[_sds((n_tok, GDN_QK_W), BF16), _sds(state_all.shape, F32), _sds((3, n_tok, GDN_CH), F32)],
        input_output_aliases=aliases,
        scratch_shapes=[pltpu.VMEM((GDN_HEADS, GDN_D, n_tok), F32),
                        pltpu.VMEM((GDN_HEADS, GDN_D, n_tok), F32),
                        pltpu.VMEM((n_tok, GDN_QK_W), F32),
                        pltpu.VMEM((GDN_HEADS, n_tok, LANES), F32),
                        pltpu.VMEM((GDN_HEADS, n_tok, LANES), F32),
                        pltpu.VMEM((GDN_HEADS, n_tok, LANES), F32),
                        pltpu.VMEM((n_tok, GDN_QK_W), F32)],
        compiler_params=_params(1),
        name="gdn_sample",
    )(*inputs)


def _ssd_sample_kernel(p_ref, buf_ref, cw_ref, cb_ref, pc_ref, d_ref, nw_ref, s_ref,
                       o_ref, so_ref, bo_ref,
                       xt_scr, xs_scr, bm_scr, cm_scr, a_scr, yt_scr):
    tg = pl.program_id(0)
    n_tok = p_ref.shape[0]

    @pl.when(tg == 0)
    def _():
        x = p_ref[:, SSD_COL_XBC:SSD_COL_XBC + SSM_XBC]
        y = (cw_ref[0:1, :] * buf_ref[0] + cw_ref[1:2, :] * buf_ref[1]
             + cw_ref[2:3, :] * buf_ref[2] + cw_ref[3:4, :] * x)
        bo_ref[0] = buf_ref[1]
        bo_ref[1] = buf_ref[2]
        bo_ref[2] = x
        xbc = _silu(y + cb_ref[...])
        xs = xbc[:, :SSM_W]
        xs_scr[...] = xs
        bm_scr[...] = xbc[:, SSM_W:SSM_W + SSM_BC]
        cm_scr[...] = xbc[:, SSM_W + SSM_BC:]
        head_lane = _iota((1, LANES), 1) < SSM_HEADS
        dt = jnp.where(head_lane, _softplus(p_ref[:, SSD_COL_DT:SSD_COL_DT + LANES] + pc_ref[1:2, :]), 0.0)
        decay = jnp.exp(dt * (-jnp.exp(pc_ref[0:1, :])))
        xdt_t = (xs * _expand_heads(dt, _expand_matrix())).T
        hi = xdt_t.astype(BF16)
        xt_scr[:, :n_tok] = hi
        xt_scr[:, n_tok:] = (xdt_t - hi.astype(F32)).astype(BF16)
        for h in range(SSM_HEADS):
            a_scr[h] = jnp.broadcast_to(decay[:, h:h + 1], (n_tok, LANES))
        yt_scr[...] = jnp.zeros_like(yt_scr)

    rows = pl.ds(pl.multiple_of(tg * TOK_GROUP, TOK_GROUP), TOK_GROUP)
    bm_blk = bm_scr[rows, :]
    cm_blk = cm_scr[rows, :]
    lane = _iota((1, LANES), 1)
    piece_tok = _iota((2 * n_tok, LANES), 0) & (n_tok - 1)
    y_tile = jnp.zeros((SSM_W, LANES), F32)
    for j in range(TOK_GROUP):
        pick = jnp.where(piece_tok == tg * TOK_GROUP + j, 1.0, 0.0).astype(BF16)
        xb = jnp.dot(xt_scr[...], pick, preferred_element_type=F32)
        prods = []
        for h in range(SSM_HEADS):
            g = h // SSM_RPG
            s = s_ref[j, h]
            a_row = a_scr[h, rows, :][j:j + 1, :]
            xcol = xb[h * SSM_P:(h + 1) * SSM_P, :]
            s_new = a_row * s + xcol * bm_blk[j:j + 1, g * SSM_N:(g + 1) * SSM_N]
            so_ref[j, h] = s_new
            prods.append(s_new * cm_blk[j:j + 1, g * SSM_N:(g + 1) * SSM_N])
        y_col = jnp.sum(jnp.concatenate(prods, axis=0), axis=1, keepdims=True)
        y_tile = jnp.where(lane == j, y_col, y_tile)
    in_group = (lane >> 3) == tg
    yt_scr[...] = jnp.where(in_group, pltpu.roll(y_tile, TOK_GROUP * tg, axis=1), yt_scr[...])

    @pl.when(tg == pl.num_programs(0) - 1)
    def _():
        y = yt_scr[...].T + d_ref[...] * xs_scr[...]
        zb = p_ref[:, SSD_COL_ZB:SSD_COL_ZB + SSM_W]
        o_ref[...] = _rms(y * _silu(zb), nw_ref[...]).astype(BF16)


def _ssd_sample(proj, buf_t, conv_w, conv_b, pc, d_row, norm_w, state_all, layer, prev_out):
    n_tok = proj.shape[0]
    const = lambda shape: pl.BlockSpec(shape, lambda g: (0,) * len(shape))
    st_spec = pl.BlockSpec((None, TOK_GROUP, SSM_HEADS, SSM_P, SSM_N), lambda g: (layer, g, 0, 0, 0))
    in_specs = [const((n_tok, SSD_PROJ_W)), const((3, n_tok, SSM_XBC)), const((4, SSM_XBC)), const((1, SSM_XBC)),
                const((8, LANES)), const((1, SSM_W)), const((1, SSM_W)), st_spec]
    kern, inputs, in_specs, aliases = _in_place_state(
        _ssd_sample_kernel, [proj, buf_t, conv_w, conv_b, pc, d_row, norm_w, state_all], in_specs, prev_out, 1)
    return pl.pallas_call(
        kern,
        grid=(n_tok // TOK_GROUP,),
        in_specs=in_specs,
        out_specs=[const((n_tok, SSM_W)), st_spec, const((3, n_tok, SSM_XBC))],
        out_shape=[_sds((n_tok, SSM_W), BF16), _sds(state_all.shape, F32), _sds((3, n_tok, SSM_XBC), F32)],
        input_output_aliases=aliases,
        scratch_shapes=[pltpu.VMEM((SSM_W, 2 * n_tok), BF16),
                        pltpu.VMEM((n_tok, SSM_W), F32),
                        pltpu.VMEM((n_tok, SSM_BC), F32),
                        pltpu.VMEM((n_tok, SSM_BC), F32),
                        pltpu.VMEM((SSM_HEADS, n_tok, LANES), F32),
                        pltpu.VMEM((SSM_W, n_tok), F32)],
        compiler_params=_params(1),
        name="ssd_sample",
    )(*inputs)


def _cfm_sample_kernel(p_ref, buf_ref, cw_ref, cb_ref, lnw_ref, lnb_ref, o_ref, bo_ref):
    glu = p_ref[:, CFM_COL_GLU:CFM_COL_GLU + 2 * CFM_W]
    u0 = glu[:, :CFM_W] * _sigmoid(glu[:, CFM_W:])
    acc = cw_ref[CFM_K - 1:CFM_K, :] * u0 + cb_ref[...]
    for j in range(CFM_K - 1):
        row = buf_ref[j]
        acc = acc + cw_ref[j:j + 1, :] * row
        if j > 0:
            bo_ref[j - 1] = row
    bo_ref[CFM_K - 2] = u0
    gate = p_ref[:, CFM_COL_GC:CFM_COL_GC + CFM_W]
    o_ref[...] = _layernorm_silu_gate(acc, gate, lnw_ref[...], lnb_ref[...]).astype(BF16)


def _cfm_sample(proj, buf_t, conv_w, conv_b, ln_w, ln_b):
    n_tok = proj.shape[0]
    const = lambda shape: pl.BlockSpec(shape, lambda i: (0,) * len(shape))
    return pl.pallas_call(
        _cfm_sample_kernel,
        grid=(1,),
        in_specs=[const((n_tok, CFM_PROJ_W)), const((CFM_K - 1, n_tok, CFM_W)), const((CFM_K, CFM_W)),
                  const((1, CFM_W)), const((1, CFM_W)), const((1, CFM_W))],
        out_specs=[const((n_tok, CFM_W)), const((CFM_K - 1, n_tok, CFM_W))],
        out_shape=[_sds((n_tok, CFM_W), BF16), _sds((CFM_K - 1, n_tok, CFM_W), F32)],
        compiler_params=_params(1),
        name="cfm_sample",
    )(proj, buf_t, conv_w, conv_b, ln_w, ln_b)


_IN_Q, _IN_ZA, _IN_BA, _IN_AA, _IN_ZB, _IN_XBC, _IN_DT, _IN_GLU, _IN_GC, _IN_GATES, _IN_END = (
    0, 1536, 2048, 2052, 2056, 3080, 4616, 4632, 5656, 6168, 9240)


def _prep_w_in(w_in):
    seg = lambda a, b: w_in[:, :, a:b]
    zeros = lambda n: jnp.zeros(w_in.shape[:2] + (n,), w_in.dtype)
    w_gdn = jnp.concatenate([seg(_IN_Q, _IN_ZA), seg(_IN_ZA, _IN_BA),
                             seg(_IN_BA, _IN_AA), zeros(LANES - GDN_HEADS),
                             seg(_IN_AA, _IN_ZB), zeros(LANES - GDN_HEADS)], axis=-1)
    w_ssd = jnp.concatenate([seg(_IN_XBC, _IN_DT), seg(_IN_ZB, _IN_XBC),
                             seg(_IN_DT, _IN_GLU), zeros(LANES - SSM_HEADS)], axis=-1)
    w_cfm = seg(_IN_GLU, _IN_GATES)
    w_gate = seg(_IN_GATES, _IN_END)
    return tuple(w.astype(BF16) for w in (w_gdn, w_ssd, w_cfm, w_gate))


def _lane_pad(v, rows=8):
    out = jnp.zeros((v[0].shape[0], rows, LANES), F32)
    for i, a in enumerate(v):
        out = out.at[:, i, :a.shape[1]].set(a)
    return out


def kernel(x_prompt, x_sample, state_gdn, state_gdn_conv, state_ssm, state_ssm_conv, state_cfm_conv,
           norm_w, w_in, gdn_conv_w, gdn_a_log, gdn_dt_bias, gdn_norm_w, gdn_w_o,
           ssm_conv_w, ssm_conv_b, ssm_a_log, ssm_dt_bias, ssm_d, ssm_norm_w, ssm_w_o,
           cfm_conv_w, cfm_conv_b, cfm_ln_w, cfm_ln_b, cfm_w_o, w_out, final_norm_w):
    depth = w_in.shape[0]
    batch, seq, _ = x_prompt.shape
    n_tok = x_sample.shape[0]
    assert n_tok == LANES and seq % PROMPT_T == 0

    w_gdn, w_ssd, w_cfm, w_gate = _prep_w_in(w_in)
    wa, wb, wc, wo = (w.astype(BF16) for w in (gdn_w_o, ssm_w_o, cfm_w_o, w_out))
    gdn_pc = _lane_pad([gdn_a_log, gdn_dt_bias])
    ssm_pc = _lane_pad([ssm_a_log, ssm_dt_bias])
    ssm_d_row = jnp.repeat(ssm_d, SSM_P, axis=1)[:, None, :]
    final_w = final_norm_w[None, :]

    hp = x_prompt.reshape(batch * seq, D_MODEL)
    hs = x_sample.reshape(n_tok, D_MODEL)
    p_states, s_states = [], []
    new_gdn_s = new_ssm_s = None
    for i in range(depth):
        final = i == depth - 1
        row = lambda a: a[i][None, :]
        nw = row(norm_w)

        gdn_p = (gdn_conv_w[i], gdn_pc[i], row(gdn_norm_w))
        ssd_p = (ssm_conv_w[i], row(ssm_conv_b), ssm_pc[i], ssm_d_row[i], row(ssm_norm_w))
        cfm_p = (cfm_conv_w[i], row(cfm_conv_b), row(cfm_ln_w), row(cfm_ln_b))
        hp, gdn_s, gdn_cv, ssm_s, ssm_cv, cfm_cv = _prompt_layer(
            hp, nw, (w_gdn, w_ssd, w_cfm, w_gate, wa, wb, wc, wo), i, final_w, gdn_p, ssd_p, cfm_p,
            batch, seq, final)
        p_states.append((gdn_s, gdn_cv[:, CONV_HIST - 3:], ssm_s, ssm_cv[:, CONV_HIST - 3:],
                         cfm_cv[:, CFM_HIST - (CFM_K - 1):]))

        oa, new_gdn_s, gdn_buf = _gdn_sample(_sample_proj(hs, nw, w_gdn, i), jnp.swapaxes(state_gdn_conv[i], 0, 1),
                                             gdn_conv_w[i], gdn_pc[i], row(gdn_norm_w), state_gdn, i, new_gdn_s)
        yb, new_ssm_s, ssm_buf = _ssd_sample(_sample_proj(hs, nw, w_ssd, i), jnp.swapaxes(state_ssm_conv[i], 0, 1),
                                             ssm_conv_w[i], row(ssm_conv_b), ssm_pc[i], ssm_d_row[i],
                                             row(ssm_norm_w), state_ssm, i, new_ssm_s)
        uc, cfm_buf = _cfm_sample(_sample_proj(hs, nw, w_cfm, i), jnp.swapaxes(state_cfm_conv[i], 0, 1),
                                  cfm_conv_w[i], row(cfm_conv_b), row(cfm_ln_w), row(cfm_ln_b))
        hs = _out_proj(hs, nw, w_gate, i, oa, yb, uc, wa, wb, wc, wo, final_w, final)
        s_states.append((jnp.swapaxes(gdn_buf, 0, 1), jnp.swapaxes(ssm_buf, 0, 1), jnp.swapaxes(cfm_buf, 0, 1)))

    stack = lambda states, j: jnp.stack([s[j] for s in states])
    return (hp.reshape(batch, seq, D_MODEL), hs.reshape(n_tok, 1, D_MODEL),
            stack(p_states, 0), stack(p_states, 1), stack(p_states, 2), stack(p_states, 3), stack(p_states, 4),
            new_gdn_s, stack(s_states, 0), new_ssm_s, stack(s_states, 1), stack(s_states, 2))
```

```python
import functools

import jax
import jax.numpy as jnp
from jax import lax
from jax.experimental import pallas as pl
from jax.experimental.pallas import tpu as pltpu

F32 = jnp.float32
BF16 = jnp.bfloat16

D_MODEL = 1024
GDN_HEADS = 4
GDN_D = 128
GDN_QK_W = GDN_HEADS * GDN_D
GDN_CH = 3 * GDN_QK_W
SSM_W = 1024
SSM_P = 64
SSM_HEADS = 16
SSM_GROUPS = 2
SSM_RPG = SSM_HEADS // SSM_GROUPS
SSM_GW = SSM_RPG * SSM_P
SSM_N = 128
SSM_BC = SSM_GROUPS * SSM_N
SSM_XBC = SSM_W + 2 * SSM_BC
CFM_W = 512
CFM_K = 31
CFM_HIST = 32
CFM_ROWS = 32
CONV_HIST = 8
CHUNK = 64
CHUNK_SHIFT = 6
EPS = 1e-6
NEG = -1e30
LANES = 128

GDN_COL_QKV, GDN_COL_ZA, GDN_COL_BA, GDN_COL_AA, GDN_PROJ_W = 0, 1536, 2048, 2176, 2304
SSD_COL_XBC, SSD_COL_ZB, SSD_COL_DT, SSD_PROJ_W = 0, 1536, 2560, 2688
CFM_COL_GLU, CFM_COL_GC, CFM_PROJ_W = 0, 1024, 1536
GATE_PROJ_W = 3 * D_MODEL

PROMPT_T = 256
VMEM_LIMIT = 56 * 1024 * 1024


def _sds(shape, dtype):
    return jax.ShapeDtypeStruct(shape, dtype)


def _params(n_axes):
    return pltpu.CompilerParams(dimension_semantics=("arbitrary",) * n_axes,
                                vmem_limit_bytes=VMEM_LIMIT)


def _sigmoid(x):
    return jax.nn.sigmoid(x)


def _silu(x):
    return x * jax.nn.sigmoid(x)


def _softplus(x):
    return jnp.maximum(x, 0.0) + jnp.log1p(jnp.exp(-jnp.abs(x)))


def _mm(a, b):
    return jnp.dot(a.astype(BF16), b.astype(BF16), preferred_element_type=F32)


def _mm_nt(a, b):
    return lax.dot_general(a.astype(BF16), b.astype(BF16), (((1,), (1,)), ((), ())),
                           preferred_element_type=F32)


def _mm_tn(a, b):
    return lax.dot_general(a.astype(BF16), b.astype(BF16), (((0,), (0,)), ((), ())),
                           preferred_element_type=F32)


def _rms(x, w):
    return x * lax.rsqrt(jnp.mean(x * x, axis=-1, keepdims=True) + EPS) * w


def _norm_proj(x_ref, nw_ref, w_ref):
    h = _rms(x_ref[...], nw_ref[...]).astype(BF16)
    return jnp.dot(h, w_ref[...], preferred_element_type=F32)


def _iota(shape, dim):
    return lax.broadcasted_iota(jnp.int32, shape, dim)


def _chunk_cumsum(x):
    pos = _iota(x.shape, 0) & (CHUNK - 1)
    d = 1
    while d < CHUNK:
        x = x + jnp.where(pos >= d, pltpu.roll(x, d, axis=0), 0.0)
        d *= 2
    return x


def _chunk_last(x):
    n = x.shape[0] // CHUNK
    return jnp.concatenate(
        [jnp.broadcast_to(x[(c + 1) * CHUNK - 1:(c + 1) * CHUNK, :], (CHUNK, x.shape[1])) for c in range(n)],
        axis=0)


def _expand_heads(x, emat):
    hi = x.astype(BF16).astype(F32)
    r1 = x - hi
    mid = r1.astype(BF16).astype(F32)
    lo = r1 - mid
    packed = hi + pltpu.roll(mid, SSM_HEADS, axis=1) + pltpu.roll(lo, 2 * SSM_HEADS, axis=1)
    return jnp.dot(packed.astype(BF16), emat, preferred_element_type=F32)


def _expand_matrix():
    r = _iota((LANES, SSM_W), 0)
    c = _iota((LANES, SSM_W), 1)
    return jnp.where(((r & (SSM_HEADS - 1)) == (c >> CHUNK_SHIFT)) & (r < 3 * SSM_HEADS), 1.0, 0.0).astype(BF16)


def _causal_conv4(x, xe_scr, w_ref):
    t_len = x.shape[0]
    xe = jnp.concatenate([xe_scr[0:CONV_HIST, :], x], axis=0)
    acc = xe * w_ref[0:1, :]
    for j in range(1, 4):
        acc = pltpu.roll(acc, 1, axis=0) + xe * w_ref[j:j + 1, :]
    new_hist = x[t_len - CONV_HIST:, :]
    xe_scr[0:CONV_HIST, :] = new_hist
    return acc[CONV_HIST:, :], new_hist


GDN_GROUP = PROMPT_T // CHUNK
GDN_CAT = GDN_GROUP * CHUNK
GDN_D_SHIFT = 7


def _lane_cat(x):
    return jnp.concatenate([x[c * CHUNK:(c + 1) * CHUNK, :] for c in range(GDN_GROUP)], axis=1)


def _gdn_pre(proj, cw_ref, pc_ref, cv_ref, xe_scr):
    y, new_hist = _causal_conv4(proj[:, GDN_COL_QKV:GDN_COL_QKV + GDN_CH], xe_scr, cw_ref)
    cv_ref[0] = new_hist
    qkv = _silu(y)
    q_n, k_n, v_n = [], [], []
    for h in range(GDN_HEADS):
        q = qkv[:, h * GDN_D:(h + 1) * GDN_D]
        k = qkv[:, GDN_QK_W + h * GDN_D:GDN_QK_W + (h + 1) * GDN_D]
        q_n.append(q * lax.rsqrt(jnp.sum(q * q, axis=-1, keepdims=True) + EPS) * (GDN_D ** -0.5))
        k_n.append(k * lax.rsqrt(jnp.sum(k * k, axis=-1, keepdims=True) + EPS))
        v_n.append(qkv[:, 2 * GDN_QK_W + h * GDN_D:2 * GDN_QK_W + (h + 1) * GDN_D])
    beta_c = _sigmoid(proj[:, GDN_COL_BA:GDN_COL_BA + LANES])
    g_c = -jnp.exp(pc_ref[0:1, :]) * _softplus(proj[:, GDN_COL_AA:GDN_COL_AA + LANES] + pc_ref[1:2, :])
    return q_n, k_n, v_n, beta_c, g_c


def _gdn_main(pre, za, nw_ref, bd_ref, km_ref, st_ref, s_scr, o_scr):
    q_n, k_n, v_n, beta_c, g_c = pre
    heads = range(GDN_HEADS)
    gam_c = _chunk_cumsum(g_c)
    glast_c = _chunk_last(gam_c)
    eg_c = jnp.exp(gam_c)
    ekd_c = jnp.exp(glast_c - gam_c)
    dtot_c = jnp.exp(glast_c)
    bg_c = beta_c * eg_c
    gam_r = gam_c.T
    col = lambda a, h: a[:, h:h + 1]

    t_i = _iota((CHUNK, GDN_CAT), 0)
    s_i = _iota((CHUNK, GDN_CAT), 1) & (CHUNK - 1)
    incl, strict = t_i >= s_i, t_i > s_i
    eye_cat = jnp.where(t_i == s_i, 1.0, 0.0)
    bd_mask = bd_ref[...]
    k_mask = km_ref[...]
    low_half = _iota((1, LANES), 1) < CHUNK

    def block_diag(y_cat):
        return jnp.concatenate([y_cat.astype(BF16)] * GDN_GROUP, axis=0) * bd_mask

    kb = [k_n[h] * col(beta_c, h) for h in heads]
    aq = []
    for h in heads:
        rhs_nt = jnp.concatenate([k_n[h].astype(BF16)] * GDN_GROUP, axis=1) * k_mask
        aq.append(_mm_nt(jnp.concatenate([_lane_cat(kb[h]), _lane_cat(q_n[h])], axis=0), rhs_nt))

    dincl = []
    for h in heads:
        gcol = col(gam_c, h)
        tiles = []
        for m in range(GDN_GROUP // 2):
            lo = jnp.broadcast_to(gcol[2 * m * CHUNK:(2 * m + 1) * CHUNK, :], (CHUNK, LANES))
            hi = jnp.broadcast_to(gcol[(2 * m + 1) * CHUNK:(2 * m + 2) * CHUNK, :], (CHUNK, LANES))
            tiles.append(jnp.where(low_half, lo, hi))
        dincl.append(jnp.exp(jnp.where(incl, jnp.concatenate(tiles, axis=1) - gam_r[h:h + 1, :], NEG)))
    n0 = [aq[h][:CHUNK, :] * jnp.where(strict, dincl[h], 0.0) for h in heads]
    attn_bd = [block_diag(aq[h][CHUNK:, :] * dincl[h]) for h in heads]

    def lower_left(m):
        span = 2 * m - 1
        return ((t_i | span) == (s_i | span)) & ((t_i & span) >= m) & ((s_i & span) < m)

    tinv = [eye_cat - jnp.where(lower_left(1), n0[h], 0.0) for h in heads]
    for m in (2, 4, 8, 16, 32):
        ll = lower_left(m)
        b_ainv = [_mm(jnp.where(ll, n0[h], 0.0), block_diag(tinv[h])) for h in heads]
        tinv = [tinv[h] - _mm(tinv[h], block_diag(b_ainv[h])) for h in heads]

    wu = [_mm(block_diag(tinv[h]),
              jnp.concatenate([k_n[h] * col(bg_c, h), v_n[h] * col(beta_c, h)], axis=1)) for h in heads]
    qg = [q_n[h] * col(eg_c, h) for h in heads]
    kd = [k_n[h] * col(ekd_c, h) for h in heads]

    state = [s_scr[h] for h in heads]
    zeros = jnp.zeros((CHUNK, GDN_D), F32)
    for c in range(GDN_GROUP):
        rows = slice(c * CHUNK, (c + 1) * CHUNK)
        ws_qs = [_mm(jnp.concatenate([wu[h][rows, :GDN_D], qg[h][rows, :]], axis=0), state[h]) for h in heads]
        u = [wu[h][rows, GDN_D:] - ws_qs[h][:CHUNK, :] for h in heads]
        for h in heads:
            u_pad = jnp.concatenate([zeros] * c + [u[h]] + [zeros] * (GDN_GROUP - 1 - c), axis=0)
            o_scr[rows, h * GDN_D:(h + 1) * GDN_D] = ws_qs[h][CHUNK:, :] + _mm(attn_bd[h][rows, :], u_pad)
        state = [state[h] * dtot_c[c * CHUNK:c * CHUNK + 1, h:h + 1] + _mm_tn(kd[h][rows, :], u[h])
                 for h in heads]
    for h in heads:
        s_scr[h] = state[h]

    o = o_scr[...]
    outs = []
    for h in heads:
        sl = slice(h * GDN_D, (h + 1) * GDN_D)
        outs.append(_rms(o[:, sl], nw_ref[...]) * _silu(za[:, sl]))
    st_ref[0] = s_scr[...]
    return jnp.concatenate(outs, axis=1).astype(BF16)


def _ssd_pre(proj, cw_ref, cb_ref, pc_ref, cv_ref, xe_scr):
    y, new_hist = _causal_conv4(proj[:, SSD_COL_XBC:SSD_COL_XBC + SSM_XBC], xe_scr, cw_ref)
    cv_ref[0] = new_hist
    xbc = _silu(y + cb_ref[...])
    head_lane = _iota((1, LANES), 1) < SSM_HEADS
    dt_c = jnp.where(head_lane, _softplus(proj[:, SSD_COL_DT:SSD_COL_DT + LANES] + pc_ref[1:2, :]), 0.0)
    return xbc, dt_c


def _ssd_main(pre, zb, pc_ref, d_ref, nw_ref, emat_ref, st_ref, st_scr, y_scr):
    xbc, dt_c = pre
    xs = xbc[:, :SSM_W]
    bm = xbc[:, SSM_W:SSM_W + SSM_BC]
    cm = xbc[:, SSM_W + SSM_BC:]
    head_lane = _iota((1, LANES), 1) < SSM_HEADS
    gam_c = _chunk_cumsum(dt_c * (-jnp.exp(pc_ref[0:1, :])))
    glast_c = _chunk_last(gam_c)
    emat = emat_ref[...]
    e1 = _expand_heads(jnp.where(head_lane, jnp.exp(gam_c), 0.0), emat)
    e2 = _expand_heads(jnp.exp(glast_c - gam_c) * dt_c, emat)
    gam_r = gam_c.T
    dt_r = dt_c.T

    t_i = _iota((CHUNK, CHUNK), 0)
    s_i = _iota((CHUNK, CHUNK), 1)
    incl = t_i >= s_i
    low_half = _iota((1, LANES), 1) < SSM_P

    for c in range(PROMPT_T // CHUNK):
        rows = slice(c * CHUNK, (c + 1) * CHUNK)
        xs_c = xs[rows, :]
        bm_c = bm[rows, :]
        cm_c = cm[rows, :]
        gam_cc = gam_c[rows, :]
        cbs = [_mm_nt(cm_c[:, g * SSM_N:(g + 1) * SSM_N], bm_c[:, g * SSM_N:(g + 1) * SSM_N])
               for g in range(SSM_GROUPS)]
        pairs = []
        for j in range(SSM_HEADS // 2):
            xp = xs_c[:, j * LANES:(j + 1) * LANES]
            acc = None
            for half in range(2):
                h = 2 * j + half
                m = (cbs[h // SSM_RPG]
                     * jnp.exp(jnp.where(incl, gam_cc[:, h:h + 1] - gam_r[h:h + 1, rows], NEG))
                     * dt_r[h:h + 1, rows])
                xm = jnp.where(low_half if half == 0 else jnp.logical_not(low_half), xp, 0.0)
                part = _mm(m, xm)
                acc = part if acc is None else acc + part
            pairs.append(acc)
        y_intra = jnp.concatenate(pairs, axis=1)
        y_inter = jnp.concatenate(
            [_mm(cm_c[:, g * SSM_N:(g + 1) * SSM_N], st_scr[g]) for g in range(SSM_GROUPS)], axis=1)
        y_scr[rows, :] = y_intra + y_inter * e1[rows, :] + d_ref[...] * xs_c
        xe = xs_c * e2[rows, :]
        dtot = e1[(c + 1) * CHUNK - 1:(c + 1) * CHUNK, :]
        for g in range(SSM_GROUPS):
            sl = slice(g * SSM_GW, (g + 1) * SSM_GW)
            st_scr[g] = st_scr[g] * dtot[:, sl] + _mm_tn(bm_c[:, g * SSM_N:(g + 1) * SSM_N], xe[:, sl])

    @pl.when(pl.program_id(1) == pl.num_programs(1) - 1)
    def _():
        for g in range(SSM_GROUPS):
            st_ref[0, g * SSM_RPG:(g + 1) * SSM_RPG] = st_scr[g].T.reshape(SSM_RPG, SSM_P, SSM_N)

    return _rms(y_scr[...] * _silu(zb), nw_ref[...]).astype(BF16)


def _layernorm_silu_gate(u, gate, lnw, lnb):
    mu = jnp.mean(u, axis=-1, keepdims=True)
    uc = u - mu
    var = jnp.mean(uc * uc, axis=-1, keepdims=True)
    return _silu(uc * lax.rsqrt(var + EPS) * lnw + lnb) * _silu(gate)


def _cfm_stage(proj, cw_ref, cb_ref, lnw_ref, lnb_ref, cv_ref, hist_scr):
    glu = proj[:, CFM_COL_GLU:CFM_COL_GLU + 2 * CFM_W]
    u0 = glu[:, :CFM_W] * _sigmoid(glu[:, CFM_W:])
    xe = jnp.concatenate([hist_scr[...], u0], axis=0)
    rolled = [xe] + [pltpu.roll(xe, b, axis=0) for b in range(1, 8)]
    new_hist = u0[PROMPT_T - CFM_HIST:, :]
    hist_scr[...] = new_hist
    cv_ref[0] = new_hist
    gate = proj[:, CFM_COL_GC:CFM_COL_GC + CFM_W]
    outs = []
    for r0 in range(0, PROMPT_T, CFM_ROWS):
        acc = u0[r0:r0 + CFM_ROWS, :] * cw_ref[CFM_K - 1:CFM_K, :]
        for s in range(1, CFM_K):
            a, b = divmod(s, 8)
            lo = CFM_HIST - 8 * a + r0
            acc = acc + rolled[b][lo:lo + CFM_ROWS, :] * cw_ref[CFM_K - 1 - s:CFM_K - s, :]
        outs.append(_layernorm_silu_gate(acc + cb_ref[...], gate[r0:r0 + CFM_ROWS, :], lnw_ref[...],
                                         lnb_ref[...]).astype(BF16))
    return jnp.concatenate(outs, axis=0)


def _merge_stage(x, gates, oa, yb, uc, wa_ref, wb_ref, wc_ref, wo_ref, fw_ref, final):
    dot = functools.partial(jnp.dot, preferred_element_type=F32)
    merged = (_sigmoid(gates[:, :D_MODEL]) * dot(oa, wa_ref[...])
              + _sigmoid(gates[:, D_MODEL:2 * D_MODEL]) * dot(yb, wb_ref[...])
              + _sigmoid(gates[:, 2 * D_MODEL:]) * dot(uc, wc_ref[...]))
    out = x + dot(merged.astype(BF16), wo_ref[...])
    return _rms(out, fw_ref[...]) if final else out


def _prompt_layer_kernel(x_ref, nwin_ref, wg_ref, ws_ref, wc_ref, wgate_ref, wa_ref, wb_ref, wcc_ref, wo_ref, fw_ref,
                         g_cw, g_pc, g_nw, s_cw, s_cb, s_pc, s_d, s_nw, c_cw, c_cb, c_lnw, c_lnb,
                         bd_ref, km_ref, emat_ref,
                         o_ref, gst_ref, gcv_ref, sst_ref, scv_ref, ccv_ref,
                         s_scr, gxe_scr, o_scr, st_scr, sxe_scr, y_scr, chist_scr, *, final):
    @pl.when(pl.program_id(1) == 0)
    def _():
        s_scr[...] = jnp.zeros_like(s_scr)
        st_scr[...] = jnp.zeros_like(st_scr)
        chist_scr[...] = jnp.zeros_like(chist_scr)
        gxe_scr[0:CONV_HIST, :] = jnp.zeros((CONV_HIST, GDN_CH), F32)
        sxe_scr[0:CONV_HIST, :] = jnp.zeros((CONV_HIST, SSM_XBC), F32)

    dot = functools.partial(jnp.dot, preferred_element_type=F32)
    x = x_ref[...]
    h = _rms(x, nwin_ref[...]).astype(BF16)
    pc = dot(h, wc_ref[...])
    pg = dot(h, wg_ref[...])
    uc = _cfm_stage(pc, c_cw, c_cb, c_lnw, c_lnb, ccv_ref, chist_scr)
    ps = dot(h, ws_ref[...])
    gdn_pre = _gdn_pre(pg, g_cw, g_pc, gcv_ref, gxe_scr)
    pgate = dot(h, wgate_ref[...])
    ssd_pre = _ssd_pre(ps, s_cw, s_cb, s_pc, scv_ref, sxe_scr)
    oa = _gdn_main(gdn_pre, pg[:, GDN_COL_ZA:GDN_COL_ZA + GDN_QK_W], g_nw, bd_ref, km_ref, gst_ref, s_scr, o_scr)
    yb = _ssd_main(ssd_pre, ps[:, SSD_COL_ZB:SSD_COL_ZB + SSM_W], s_pc, s_d, s_nw, emat_ref, sst_ref, st_scr,
                   y_scr)
    o_ref[...] = _merge_stage(x, pgate, oa, yb, uc, wa_ref, wb_ref, wcc_ref, wo_ref, fw_ref, final)


def _prompt_layer(x, norm_w, weights, layer, final_w, gdn_p, ssd_p, cfm_p, batch, seq, final):
    nt = seq // PROMPT_T
    tok = lambda width: pl.BlockSpec((PROMPT_T, width), lambda b, t: (b * nt + t, 0))
    const = lambda a: pl.BlockSpec(a.shape, lambda b, t: (0,) * a.ndim)
    resident = lambda a: pl.BlockSpec((None,) + a.shape[1:], lambda b, t: (layer,) + (0,) * (a.ndim - 1),
                                      pipeline_mode=pl.Buffered(1))
    per_seq = lambda *shape: pl.BlockSpec((1,) + shape, lambda b, t: (b,) + (0,) * len(shape))
    chunk_of = jnp.arange(GDN_CAT) // CHUNK
    bd_mask = (chunk_of[:, None] == chunk_of[None, :]).astype(BF16)
    k_mask = (chunk_of[:, None] == (jnp.arange(GDN_GROUP * GDN_D) // GDN_D)[None, :]).astype(BF16)
    small = [final_w] + list(gdn_p) + list(ssd_p) + list(cfm_p) + [bd_mask, k_mask, _expand_matrix()]
    inputs = [x, norm_w] + list(weights) + small
    in_specs = ([tok(D_MODEL), const(norm_w)] + [resident(w) for w in weights] + [const(a) for a in small])
    return pl.pallas_call(
        functools.partial(_prompt_layer_kernel, final=final),
        grid=(batch, nt),
        in_specs=in_specs,
        out_specs=[tok(D_MODEL),
                   per_seq(GDN_HEADS, GDN_D, GDN_D), per_seq(CONV_HIST, GDN_CH),
                   per_seq(SSM_HEADS, SSM_P, SSM_N), per_seq(CONV_HIST, SSM_XBC),
                   per_seq(CFM_HIST, CFM_W)],
        out_shape=[_sds((batch * seq, D_MODEL), F32),
                   _sds((batch, GDN_HEADS, GDN_D, GDN_D), F32), _sds((batch, CONV_HIST, GDN_CH), F32),
                   _sds((batch, SSM_HEADS, SSM_P, SSM_N), F32), _sds((batch, CONV_HIST, SSM_XBC), F32),
                   _sds((batch, CFM_HIST, CFM_W), F32)],
        scratch_shapes=[pltpu.VMEM((GDN_HEADS, GDN_D, GDN_D), F32),
                        pltpu.VMEM((CONV_HIST + PROMPT_T, GDN_CH), F32),
                        pltpu.VMEM((PROMPT_T, GDN_QK_W), F32),
                        pltpu.VMEM((SSM_GROUPS, SSM_N, SSM_GW), F32),
                        pltpu.VMEM((CONV_HIST + PROMPT_T, SSM_XBC), F32),
                        pltpu.VMEM((PROMPT_T, SSM_W), F32),
                        pltpu.VMEM((CFM_HIST, CFM_W), F32)],
        compiler_params=_params(2),
        name="prompt_layer",
    )(*inputs)


def _out_kernel(x_ref, nwin_ref, wg_ref, oa_ref, yb_ref, uc_ref, wa_ref, wb_ref, wc_ref, wo_ref, fw_ref,
                o_ref, *, final):
    x = x_ref[...]
    gates = jnp.dot(_rms(x, nwin_ref[...]).astype(BF16), wg_ref[...], preferred_element_type=F32)
    o_ref[...] = _merge_stage(x, gates, oa_ref[...], yb_ref[...], uc_ref[...],
                              wa_ref, wb_ref, wc_ref, wo_ref, fw_ref, final)


def _out_proj(x, norm_w, w_gate, layer, oa, yb, uc, wa, wb, wc, wo, final_w, final):
    m = x.shape[0]
    const = lambda shape: pl.BlockSpec(shape, lambda i: (0,) * len(shape))
    lw = lambda shape: pl.BlockSpec((None,) + shape, lambda i: (layer,) + (0,) * len(shape))
    return pl.pallas_call(
        functools.partial(_out_kernel, final=final),
        grid=(1,),
        in_specs=[const((m, D_MODEL)), const((1, D_MODEL)), lw((D_MODEL, GATE_PROJ_W)),
                  const((m, GDN_QK_W)), const((m, SSM_W)), const((m, CFM_W)),
                  lw((GDN_QK_W, D_MODEL)), lw((SSM_W, D_MODEL)), lw((CFM_W, D_MODEL)),
                  lw((D_MODEL, D_MODEL)), const((1, D_MODEL))],
        out_specs=const((m, D_MODEL)),
        out_shape=_sds((m, D_MODEL), F32),
        compiler_params=_params(1),
        name="merge_out",
    )(x, norm_w, w_gate, oa, yb, uc, wa, wb, wc, wo, final_w)


SAMPLE_TN = 384


def _proj_kernel(x_ref, nwin_ref, w_ref, o_ref):
    o_ref[...] = _norm_proj(x_ref, nwin_ref, w_ref)


def _sample_proj(x, norm_w, w, layer):
    m = x.shape[0]
    width = w.shape[-1]
    return pl.pallas_call(
        _proj_kernel,
        grid=(width // SAMPLE_TN,),
        in_specs=[pl.BlockSpec((m, D_MODEL), lambda j: (0, 0)),
                  pl.BlockSpec((1, D_MODEL), lambda j: (0, 0)),
                  pl.BlockSpec((None, D_MODEL, SAMPLE_TN), lambda j: (layer, 0, j))],
        out_specs=pl.BlockSpec((m, SAMPLE_TN), lambda j: (0, j)),
        out_shape=_sds((m, width), F32),
        compiler_params=_params(1),
        name="sample_proj",
    )(x, norm_w, w)


TOK_GROUP = 8


def _gdn_sample_kernel(p_ref, buf_ref, cw_ref, pc_ref, nw_ref, s_ref,
                       o_ref, so_ref, bo_ref,
                       kt_scr, qt_scr, v_scr, a_scr, b_scr, qk_scr, o_scr):
    tg = pl.program_id(0)

    @pl.when(tg == 0)
    def _():
        x = p_ref[:, GDN_COL_QKV:GDN_COL_QKV + GDN_CH]
        y = (cw_ref[0:1, :] * buf_ref[0] + cw_ref[1:2, :] * buf_ref[1]
             + cw_ref[2:3, :] * buf_ref[2] + cw_ref[3:4, :] * x)
        bo_ref[0] = buf_ref[1]
        bo_ref[1] = buf_ref[2]
        bo_ref[2] = x
        qkv = _silu(y)
        beta = _sigmoid(p_ref[:, GDN_COL_BA:GDN_COL_BA + LANES])
        decay = jnp.exp(-jnp.exp(pc_ref[0:1, :])
                        * _softplus(p_ref[:, GDN_COL_AA:GDN_COL_AA + LANES] + pc_ref[1:2, :]))
        n_tok = x.shape[0]
        for h in range(GDN_HEADS):
            q = qkv[:, h * GDN_D:(h + 1) * GDN_D]
            k = qkv[:, GDN_QK_W + h * GDN_D:GDN_QK_W + (h + 1) * GDN_D]
            q = q * lax.rsqrt(jnp.sum(q * q, axis=-1, keepdims=True) + EPS) * (GDN_D ** -0.5)
            k = k * lax.rsqrt(jnp.sum(k * k, axis=-1, keepdims=True) + EPS)
            kt_scr[h] = k.T
            qt_scr[h] = q.T
            a_scr[h] = jnp.broadcast_to(decay[:, h:h + 1], (n_tok, LANES))
            b_scr[h] = jnp.broadcast_to(beta[:, h:h + 1], (n_tok, LANES))
            qk_scr[h] = jnp.broadcast_to(jnp.sum(q * k, axis=-1, keepdims=True), (n_tok, LANES))
        v_scr[...] = qkv[:, 2 * GDN_QK_W:]

    shift = (LANES - TOK_GROUP * tg) & (LANES - 1)
    rows = pl.ds(pl.multiple_of(tg * TOK_GROUP, TOK_GROUP), TOK_GROUP)
    for h in range(GDN_HEADS):
        ktg = pltpu.roll(kt_scr[h], shift, axis=1)
        qtg = pltpu.roll(qt_scr[h], shift, axis=1)
        a_blk = a_scr[h, rows, :]
        b_blk = b_scr[h, rows, :]
        qk_blk = qk_scr[h, rows, :]
        v_blk = v_scr[rows, h * GDN_D:(h + 1) * GDN_D]
        o_rows = []
        for j in range(TOK_GROUP):
            s = s_ref[j, h]
            kcol = ktg[:, j:j + 1]
            qcol = qtg[:, j:j + 1]
            ks = jnp.sum(s * kcol, axis=0, keepdims=True)
            qs = jnp.sum(s * qcol, axis=0, keepdims=True)
            a_row = a_blk[j:j + 1, :]
            delta = b_blk[j:j + 1, :] * (v_blk[j:j + 1, :] - a_row * ks)
            so_ref[j, h] = a_row * s + kcol * delta
            o_rows.append(a_row * qs + qk_blk[j:j + 1, :] * delta)
        o_scr[rows, h * GDN_D:(h + 1) * GDN_D] = jnp.concatenate(o_rows, axis=0)

    @pl.when(tg == pl.num_programs(0) - 1)
    def _():
        o = o_scr[...]
        za = p_ref[:, GDN_COL_ZA:GDN_COL_ZA + GDN_QK_W]
        outs = []
        for h in range(GDN_HEADS):
            sl = slice(h * GDN_D, (h + 1) * GDN_D)
            outs.append(_rms(o[:, sl], nw_ref[...]) * _silu(za[:, sl]))
        o_ref[...] = jnp.concatenate(outs, axis=1).astype(BF16)


def _in_place_state(kernel_fn, inputs, in_specs, prev_out, out_index):
    if prev_out is None:
        return kernel_fn, inputs, in_specs, {}
    n = len(inputs)
    wrapped = lambda *refs: kernel_fn(*refs[:n], *refs[n + 1:])
    return wrapped, inputs + [prev_out], in_specs + [pl.BlockSpec(memory_space=pl.ANY)], {n: out_index}


def _gdn_sample(proj, buf_t, conv_w, pc, norm_w, state_all, layer, prev_out):
    n_tok = proj.shape[0]
    const = lambda shape: pl.BlockSpec(shape, lambda g: (0,) * len(shape))
    st_spec = pl.BlockSpec((None, TOK_GROUP, GDN_HEADS, GDN_D, GDN_D), lambda g: (layer, g, 0, 0, 0))
    in_specs = [const((n_tok, GDN_PROJ_W)), const((3, n_tok, GDN_CH)), const((4, GDN_CH)),
                const((8, LANES)), const((1, GDN_D)), st_spec]
    kern, inputs, in_specs, aliases = _in_place_state(
        _gdn_sample_kernel, [proj, buf_t, conv_w, pc, norm_w, state_all], in_specs, prev_out, 1)
    return pl.pallas_call(
        kern,
        grid=(n_tok // TOK_GROUP,),
        in_specs=in_specs,
        out_specs=[const((n_tok, GDN_QK_W)), st_spec, const((3, n_tok, GDN_CH))],
        out_shape=[_sds((n_tok, GDN_QK_W), BF16), _sds(state_all.shape, F32), _sds((3, n_tok, GDN_CH), F32)],
        input_output_aliases=aliases,
        scratch_shapes=[pltpu.VMEM((GDN_HEADS, GDN_D, n_tok), F32),
                        pltpu.VMEM((GDN_HEADS, GDN_D, n_tok), F32),
                        pltpu.VMEM((n_tok, GDN_QK_W), F32),
                        pltpu.VMEM((GDN_HEADS, n_tok, LANES), F32),
                        pltpu.VMEM((GDN_HEADS, n_tok, LANES), F32),
                        pltpu.VMEM((GDN_HEADS, n_tok, LANES), F32),
                        pltpu.VMEM((n_tok, GDN_QK_W), F32)],
        compiler_params=_params(1),
        name="gdn_sample",
    )(*inputs)


def _ssd_sample_kernel(p_ref, buf_ref, cw_ref, cb_ref, pc_ref, d_ref, nw_ref, s_ref,
                       o_ref, so_ref, bo_ref,
                       xt_scr, xs_scr, bm_scr, cm_scr, a_scr, yt_scr):
    tg = pl.program_id(0)
    n_tok = p_ref.shape[0]

    @pl.when(tg == 0)
    def _():
        x = p_ref[:, SSD_COL_XBC:SSD_COL_XBC + SSM_XBC]
        y = (cw_ref[0:1, :] * buf_ref[0] + cw_ref[1:2, :] * buf_ref[1]
             + cw_ref[2:3, :] * buf_ref[2] + cw_ref[3:4, :] * x)
        bo_ref[0] = buf_ref[1]
        bo_ref[1] = buf_ref[2]
        bo_ref[2] = x
        xbc = _silu(y + cb_ref[...])
        xs = xbc[:, :SSM_W]
        xs_scr[...] = xs
        bm_scr[...] = xbc[:, SSM_W:SSM_W + SSM_BC]
        cm_scr[...] = xbc[:, SSM_W + SSM_BC:]
        head_lane = _iota((1, LANES), 1) < SSM_HEADS
        dt = jnp.where(head_lane, _softplus(p_ref[:, SSD_COL_DT:SSD_COL_DT + LANES] + pc_ref[1:2, :]), 0.0)
        decay = jnp.exp(dt * (-jnp.exp(pc_ref[0:1, :])))
        xdt_t = (xs * _expand_heads(dt, _expand_matrix())).T
        hi = xdt_t.astype(BF16)
        xt_scr[:, :n_tok] = hi
        xt_scr[:, n_tok:] = (xdt_t - hi.astype(F32)).astype(BF16)
        for h in range(SSM_HEADS):
            a_scr[h] = jnp.broadcast_to(decay[:, h:h + 1], (n_tok, LANES))
        yt_scr[...] = jnp.zeros_like(yt_scr)

    rows = pl.ds(pl.multiple_of(tg * TOK_GROUP, TOK_GROUP), TOK_GROUP)
    bm_blk = bm_scr[rows, :]
    cm_blk = cm_scr[rows, :]
    lane = _iota((1, LANES), 1)
    piece_tok = _iota((2 * n_tok, LANES), 0) & (n_tok - 1)
    y_tile = jnp.zeros((SSM_W, LANES), F32)
    for j in range(TOK_GROUP):
        pick = jnp.where(piece_tok == tg * TOK_GROUP + j, 1.0, 0.0).astype(BF16)
        xb = jnp.dot(xt_scr[...], pick, preferred_element_type=F32)
        prods = []
        for h in range(SSM_HEADS):
            g = h // SSM_RPG
            s = s_ref[j, h]
            a_row = a_scr[h, rows, :][j:j + 1, :]
            xcol = xb[h * SSM_P:(h + 1) * SSM_P, :]
            s_new = a_row * s + xcol * bm_blk[j:j + 1, g * SSM_N:(g + 1) * SSM_N]
            so_ref[j, h] = s_new
            prods.append(s_new * cm_blk[j:j + 1, g * SSM_N:(g + 1) * SSM_N])
        y_col = jnp.sum(jnp.concatenate(prods, axis=0), axis=1, keepdims=True)
        y_tile = jnp.where(lane == j, y_col, y_tile)
    in_group = (lane >> 3) == tg
    yt_scr[...] = jnp.where(in_group, pltpu.roll(y_tile, TOK_GROUP * tg, axis=1), yt_scr[...])

    @pl.when(tg == pl.num_programs(0) - 1)
    def _():
        y = yt_scr[...].T + d_ref[...] * xs_scr[...]
        zb = p_ref[:, SSD_COL_ZB:SSD_COL_ZB + SSM_W]
        o_ref[...] = _rms(y * _silu(zb), nw_ref[...]).astype(BF16)


def _ssd_sample(proj, buf_t, conv_w, conv_b, pc, d_row, norm_w, state_all, layer, prev_out):
    n_tok = proj.shape[0]
    const = lambda shape: pl.BlockSpec(shape, lambda g: (0,) * len(shape))
    st_spec = pl.BlockSpec((None, TOK_GROUP, SSM_HEADS, SSM_P, SSM_N), lambda g: (layer, g, 0, 0, 0))
    in_specs = [const((n_tok, SSD_PROJ_W)), const((3, n_tok, SSM_XBC)), const((4, SSM_XBC)), const((1, SSM_XBC)),
                const((8, LANES)), const((1, SSM_W)), const((1, SSM_W)), st_spec]
    kern, inputs, in_specs, aliases = _in_place_state(
        _ssd_sample_kernel, [proj, buf_t, conv_w, conv_b, pc, d_row, norm_w, state_all], in_specs, prev_out, 1)
    return pl.pallas_call(
        kern,
        grid=(n_tok // TOK_GROUP,),
        in_specs=in_specs,
        out_specs=[const((n_tok, SSM_W)), st_spec, const((3, n_tok, SSM_XBC))],
        out_shape=[_sds((n_tok, SSM_W), BF16), _sds(state_all.shape, F32), _sds((3, n_tok, SSM_XBC), F32)],
        input_output_aliases=aliases,
        scratch_shapes=[pltpu.VMEM((SSM_W, 2 * n_tok), BF16),
                        pltpu.VMEM((n_tok, SSM_W), F32),
                        pltpu.VMEM((n_tok, SSM_BC), F32),
                        pltpu.VMEM((n_tok, SSM_BC), F32),
                        pltpu.VMEM((SSM_HEADS, n_tok, LANES), F32),
                        pltpu.VMEM((SSM_W, n_tok), F32)],
        compiler_params=_params(1),
        name="ssd_sample",
    )(*inputs)


def _cfm_sample_kernel(p_ref, buf_ref, cw_ref, cb_ref, lnw_ref, lnb_ref, o_ref, bo_ref):
    glu = p_ref[:, CFM_COL_GLU:CFM_COL_GLU + 2 * CFM_W]
    u0 = glu[:, :CFM_W] * _sigmoid(glu[:, CFM_W:])
    acc = cw_ref[CFM_K - 1:CFM_K, :] * u0 + cb_ref[...]
    for j in range(CFM_K - 1):
        row = buf_ref[j]
        acc = acc + cw_ref[j:j + 1, :] * row
        if j > 0:
            bo_ref[j - 1] = row
    bo_ref[CFM_K - 2] = u0
    gate = p_ref[:, CFM_COL_GC:CFM_COL_GC + CFM_W]
    o_ref[...] = _layernorm_silu_gate(acc, gate, lnw_ref[...], lnb_ref[...]).astype(BF16)


def _cfm_sample(proj, buf_t, conv_w, conv_b, ln_w, ln_b):
    n_tok = proj.shape[0]
    const = lambda shape: pl.BlockSpec(shape, lambda i: (0,) * len(shape))
    return pl.pallas_call(
        _cfm_sample_kernel,
        grid=(1,),
        in_specs=[const((n_tok, CFM_PROJ_W)), const((CFM_K - 1, n_tok, CFM_W)), const((CFM_K, CFM_W)),
                  const((1, CFM_W)), const((1, CFM_W)), const((1, CFM_W))],
        out_specs=[const((n_tok, CFM_W)), const((CFM_K - 1, n_tok, CFM_W))],
        out_shape=[_sds((n_tok, CFM_W), BF16), _sds((CFM_K - 1, n_tok, CFM_W), F32)],
        compiler_params=_params(1),
        name="cfm_sample",
    )(proj, buf_t, conv_w, conv_b, ln_w, ln_b)


_IN_Q, _IN_ZA, _IN_BA, _IN_AA, _IN_ZB, _IN_XBC, _IN_DT, _IN_GLU, _IN_GC, _IN_GATES, _IN_END = (
    0, 1536, 2048, 2052, 2056, 3080, 4616, 4632, 5656, 6168, 9240)


def _prep_w_in(w_in):
    seg = lambda a, b: w_in[:, :, a:b]
    zeros = lambda n: jnp.zeros(w_in.shape[:2] + (n,), w_in.dtype)
    w_gdn = jnp.concatenate([seg(_IN_Q, _IN_ZA), seg(_IN_ZA, _IN_BA),
                             seg(_IN_BA, _IN_AA), zeros(LANES - GDN_HEADS),
                             seg(_IN_AA, _IN_ZB), zeros(LANES - GDN_HEADS)], axis=-1)
    w_ssd = jnp.concatenate([seg(_IN_XBC, _IN_DT), seg(_IN_ZB, _IN_XBC),
                             seg(_IN_DT, _IN_GLU), zeros(LANES - SSM_HEADS)], axis=-1)
    w_cfm = seg(_IN_GLU, _IN_GATES)
    w_gate = seg(_IN_GATES, _IN_END)
    return tuple(w.astype(BF16) for w in (w_gdn, w_ssd, w_cfm, w_gate))


def _lane_pad(v, rows=8):
    out = jnp.zeros((v[0].shape[0], rows, LANES), F32)
    for i, a in enumerate(v):
        out = out.at[:, i, :a.shape[1]].set(a)
    return out


def kernel(x_prompt, x_sample, state_gdn, state_gdn_conv, state_ssm, state_ssm_conv, state_cfm_conv,
           norm_w, w_in, gdn_conv_w, gdn_a_log, gdn_dt_bias, gdn_norm_w, gdn_w_o,
           ssm_conv_w, ssm_conv_b, ssm_a_log, ssm_dt_bias, ssm_d, ssm_norm_w, ssm_w_o,
           cfm_conv_w, cfm_conv_b, cfm_ln_w, cfm_ln_b, cfm_w_o, w_out, final_norm_w):
    depth = w_in.shape[0]
    batch, seq, _ = x_prompt.shape
    n_tok = x_sample.shape[0]
    assert n_tok == LANES and seq % PROMPT_T == 0

    w_gdn, w_ssd, w_cfm, w_gate = _prep_w_in(w_in)
    wa, wb, wc, wo = (w.astype(BF16) for w in (gdn_w_o, ssm_w_o, cfm_w_o, w_out))
    gdn_pc = _lane_pad([gdn_a_log, gdn_dt_bias])
    ssm_pc = _lane_pad([ssm_a_log, ssm_dt_bias])
    ssm_d_row = jnp.repeat(ssm_d, SSM_P, axis=1)[:, None, :]
    final_w = final_norm_w[None, :]

    hp = x_prompt.reshape(batch * seq, D_MODEL)
    hs = x_sample.reshape(n_tok, D_MODEL)
    p_states, s_states = [], []
    new_gdn_s = new_ssm_s = None
    for i in range(depth):
        final = i == depth - 1
        row = lambda a: a[i][None, :]
        nw = row(norm_w)

        gdn_p = (gdn_conv_w[i], gdn_pc[i], row(gdn_norm_w))
        ssd_p = (ssm_conv_w[i], row(ssm_conv_b), ssm_pc[i], ssm_d_row[i], row(ssm_norm_w))
        cfm_p = (cfm_conv_w[i], row(cfm_conv_b), row(cfm_ln_w), row(cfm_ln_b))
        hp, gdn_s, gdn_cv, ssm_s, ssm_cv, cfm_cv = _prompt_layer(
            hp, nw, (w_gdn, w_ssd, w_cfm, w_gate, wa, wb, wc, wo), i, final_w, gdn_p, ssd_p, cfm_p,
            batch, seq, final)
        p_states.append((gdn_s, gdn_cv[:, CONV_HIST - 3:], ssm_s, ssm_cv[:, CONV_HIST - 3:],
                         cfm_cv[:, CFM_HIST - (CFM_K - 1):]))

        oa, new_gdn_s, gdn_buf = _gdn_sample(_sample_proj(hs, nw, w_gdn, i), jnp.swapaxes(state_gdn_conv[i], 0, 1),
                                             gdn_conv_w[i], gdn_pc[i], row(gdn_norm_w), state_gdn, i, new_gdn_s)
        yb, new_ssm_s, ssm_buf = _ssd_sample(_sample_proj(hs, nw, w_ssd, i), jnp.swapaxes(state_ssm_conv[i], 0, 1),
                                             ssm_conv_w[i], row(ssm_conv_b), ssm_pc[i], ssm_d_row[i],
                                             row(ssm_norm_w), state_ssm, i, new_ssm_s)
        uc, cfm_buf = _cfm_sample(_sample_proj(hs, nw, w_cfm, i), jnp.swapaxes(state_cfm_conv[i], 0, 1),
                                  cfm_conv_w[i], row(cfm_conv_b), row(cfm_ln_w), row(cfm_ln_b))
        hs = _out_proj(hs, nw, w_gate, i, oa, yb, uc, wa, wb, wc, wo, final_w, final)
        s_states.append((jnp.swapaxes(gdn_buf, 0, 1), jnp.swapaxes(ssm_buf, 0, 1), jnp.swapaxes(cfm_buf, 0, 1)))

    stack = lambda states, j: jnp.stack([s[j] for s in states])
    return (hp.reshape(batch, seq, D_MODEL), hs.reshape(n_tok, 1, D_MODEL),
            stack(p_states, 0), stack(p_states, 1), stack(p_states, 2), stack(p_states, 3), stack(p_states, 4),
            new_gdn_s, stack(s_states, 0), new_ssm_s, stack(s_states, 1), stack(s_states, 2))
```

```python
import functools

import jax
import jax.numpy as jnp
from jax import lax
from jax.experimental import pallas as pl
from jax.experimental.pallas import tpu as pltpu

F32 = jnp.float32
BF16 = jnp.bfloat16

D_MODEL = 1024
GDN_HEADS = 4
GDN_D = 128
GDN_QK_W = GDN_HEADS * GDN_D
GDN_CH = 3 * GDN_QK_W
SSM_W = 1024
SSM_P = 64
SSM_HEADS = 16
SSM_GROUPS = 2
SSM_RPG = SSM_HEADS // SSM_GROUPS
SSM_GW = SSM_RPG * SSM_P
SSM_N = 128
SSM_BC = SSM_GROUPS * SSM_N
SSM_XBC = SSM_W + 2 * SSM_BC
CFM_W = 512
CFM_K = 31
CFM_HIST = 32
CFM_ROWS = 32
CONV_HIST = 8
CHUNK = 64
CHUNK_SHIFT = 6
EPS = 1e-6
NEG = -1e30
LANES = 128

GDN_COL_QKV, GDN_COL_ZA, GDN_COL_BA, GDN_COL_AA, GDN_PROJ_W = 0, 1536, 2048, 2176, 2304
SSD_COL_XBC, SSD_COL_ZB, SSD_COL_DT, SSD_PROJ_W = 0, 1536, 2560, 2688
CFM_COL_GLU, CFM_COL_GC, CFM_PROJ_W = 0, 1024, 1536
GATE_PROJ_W = 3 * D_MODEL

PROMPT_T = 256
VMEM_LIMIT = 56 * 1024 * 1024


def _sds(shape, dtype):
    return jax.ShapeDtypeStruct(shape, dtype)


def _params(n_axes):
    return pltpu.CompilerParams(dimension_semantics=("arbitrary",) * n_axes,
                                vmem_limit_bytes=VMEM_LIMIT)


def _sigmoid(x):
    return jax.nn.sigmoid(x)


def _silu(x):
    return x * jax.nn.sigmoid(x)


def _softplus(x):
    return jnp.maximum(x, 0.0) + jnp.log1p(jnp.exp(-jnp.abs(x)))


def _mm(a, b):
    return jnp.dot(a.astype(BF16), b.astype(BF16), preferred_element_type=F32)


def _mm_nt(a, b):
    return lax.dot_general(a.astype(BF16), b.astype(BF16), (((1,), (1,)), ((), ())),
                           preferred_element_type=F32)


def _mm_tn(a, b):
    return lax.dot_general(a.astype(BF16), b.astype(BF16), (((0,), (0,)), ((), ())),
                           preferred_element_type=F32)


def _rms(x, w):
    return x * lax.rsqrt(jnp.mean(x * x, axis=-1, keepdims=True) + EPS) * w


def _norm_proj(x_ref, nw_ref, w_ref):
    h = _rms(x_ref[...], nw_ref[...]).astype(BF16)
    return jnp.dot(h, w_ref[...], preferred_element_type=F32)


def _iota(shape, dim):
    return lax.broadcasted_iota(jnp.int32, shape, dim)


def _chunk_cumsum(x):
    pos = _iota(x.shape, 0) & (CHUNK - 1)
    d = 1
    while d < CHUNK:
        x = x + jnp.where(pos >= d, pltpu.roll(x, d, axis=0), 0.0)
        d *= 2
    return x


def _chunk_last(x):
    n = x.shape[0] // CHUNK
    return jnp.concatenate(
        [jnp.broadcast_to(x[(c + 1) * CHUNK - 1:(c + 1) * CHUNK, :], (CHUNK, x.shape[1])) for c in range(n)],
        axis=0)


def _expand_heads(x, emat):
    hi = x.astype(BF16).astype(F32)
    r1 = x - hi
    mid = r1.astype(BF16).astype(F32)
    lo = r1 - mid
    packed = hi + pltpu.roll(mid, SSM_HEADS, axis=1) + pltpu.roll(lo, 2 * SSM_HEADS, axis=1)
    return jnp.dot(packed.astype(BF16), emat, preferred_element_type=F32)


def _expand_matrix():
    r = _iota((LANES, SSM_W), 0)
    c = _iota((LANES, SSM_W), 1)
    return jnp.where(((r & (SSM_HEADS - 1)) == (c >> CHUNK_SHIFT)) & (r < 3 * SSM_HEADS), 1.0, 0.0).astype(BF16)


def _causal_conv4(x, xe_scr, w_ref):
    t_len = x.shape[0]
    xe = jnp.concatenate([xe_scr[0:CONV_HIST, :], x], axis=0)
    acc = xe * w_ref[0:1, :]
    for j in range(1, 4):
        acc = pltpu.roll(acc, 1, axis=0) + xe * w_ref[j:j + 1, :]
    new_hist = x[t_len - CONV_HIST:, :]
    xe_scr[0:CONV_HIST, :] = new_hist
    return acc[CONV_HIST:, :], new_hist


GDN_GROUP = PROMPT_T // CHUNK
GDN_CAT = GDN_GROUP * CHUNK


def _lane_cat(x):
    return jnp.concatenate([x[c * CHUNK:(c + 1) * CHUNK, :] for c in range(GDN_GROUP)], axis=1)


def _gdn_pre(proj, cw_ref, pc_ref, cv_ref, xe_scr):
    y, new_hist = _causal_conv4(proj[:, GDN_COL_QKV:GDN_COL_QKV + GDN_CH], xe_scr, cw_ref)
    cv_ref[0] = new_hist
    qkv = _silu(y)
    q_n, k_n, v_n = [], [], []
    for h in range(GDN_HEADS):
        q = qkv[:, h * GDN_D:(h + 1) * GDN_D]
        k = qkv[:, GDN_QK_W + h * GDN_D:GDN_QK_W + (h + 1) * GDN_D]
        q_n.append(q * lax.rsqrt(jnp.sum(q * q, axis=-1, keepdims=True) + EPS) * (GDN_D ** -0.5))
        k_n.append(k * lax.rsqrt(jnp.sum(k * k, axis=-1, keepdims=True) + EPS))
        v_n.append(qkv[:, 2 * GDN_QK_W + h * GDN_D:2 * GDN_QK_W + (h + 1) * GDN_D])
    beta_c = _sigmoid(proj[:, GDN_COL_BA:GDN_COL_BA + LANES])
    g_c = -jnp.exp(pc_ref[0:1, :]) * _softplus(proj[:, GDN_COL_AA:GDN_COL_AA + LANES] + pc_ref[1:2, :])
    return q_n, k_n, v_n, beta_c, g_c


def _gdn_main(pre, za, nw_ref, bd_ref, km_ref, st_ref, s_scr, o_scr):
    q_n, k_n, v_n, beta_c, g_c = pre
    heads = range(GDN_HEADS)
    gam_c = _chunk_cumsum(g_c)
    glast_c = _chunk_last(gam_c)
    eg_c = jnp.exp(gam_c)
    ekd_c = jnp.exp(glast_c - gam_c)
    dtot_c = jnp.exp(glast_c)
    bg_c = beta_c * eg_c
    gam_r = gam_c.T
    col = lambda a, h: a[:, h:h + 1]

    t_i = _iota((CHUNK, GDN_CAT), 0)
    s_i = _iota((CHUNK, GDN_CAT), 1) & (CHUNK - 1)
    incl, strict = t_i >= s_i, t_i > s_i
    eye_cat = jnp.where(t_i == s_i, 1.0, 0.0)
    bd_mask = bd_ref[...]
    k_mask = km_ref[...]
    low_half = _iota((1, LANES), 1) < CHUNK

    def block_diag(y_cat):
        return jnp.concatenate([y_cat.astype(BF16)] * GDN_GROUP, axis=0) * bd_mask

    kb = [k_n[h] * col(beta_c, h) for h in heads]
    aq = []
    for h in heads:
        rhs_nt = jnp.concatenate([k_n[h].astype(BF16)] * GDN_GROUP, axis=1) * k_mask
        aq.append(_mm_nt(jnp.concatenate([_lane_cat(kb[h]), _lane_cat(q_n[h])], axis=0), rhs_nt))

    dincl = []
    for h in heads:
        gcol = col(gam_c, h)
        tiles = []
        for m in range(GDN_GROUP // 2):
            lo = jnp.broadcast_to(gcol[2 * m * CHUNK:(2 * m + 1) * CHUNK, :], (CHUNK, LANES))
            hi = jnp.broadcast_to(gcol[(2 * m + 1) * CHUNK:(2 * m + 2) * CHUNK, :], (CHUNK, LANES))
            tiles.append(jnp.where(low_half, lo, hi))
        dincl.append(jnp.exp(jnp.where(incl, jnp.concatenate(tiles, axis=1) - gam_r[h:h + 1, :], NEG)))
    n0 = [aq[h][:CHUNK, :] * jnp.where(strict, dincl[h], 0.0) for h in heads]
    attn_bd = [block_diag(aq[h][CHUNK:, :] * dincl[h]) for h in heads]

    def lower_left(m):
        span = 2 * m - 1
        return ((t_i | span) == (s_i | span)) & ((t_i & span) >= m) & ((s_i & span) < m)

    tinv = [eye_cat - jnp.where(lower_left(1), n0[h], 0.0) for h in heads]
    for m in (2, 4, 8, 16, 32):
        ll = lower_left(m)
        b_ainv = [_mm(jnp.where(ll, n0[h], 0.0), block_diag(tinv[h])) for h in heads]
        tinv = [tinv[h] - _mm(tinv[h], block_diag(b_ainv[h])) for h in heads]

    wu = [_mm(block_diag(tinv[h]),
              jnp.concatenate([k_n[h] * col(bg_c, h), v_n[h] * col(beta_c, h)], axis=1)) for h in heads]
    qg = [q_n[h] * col(eg_c, h) for h in heads]
    kd = [k_n[h] * col(ekd_c, h) for h in heads]

    state = [s_scr[h] for h in heads]
    zeros = jnp.zeros((CHUNK, GDN_D), F32)
    for c in range(GDN_GROUP):
        rows = slice(c * CHUNK, (c + 1) * CHUNK)
        ws_qs = [_mm(jnp.concatenate([wu[h][rows, :GDN_D], qg[h][rows, :]], axis=0), state[h]) for h in heads]
        u = [wu[h][rows, GDN_D:] - ws_qs[h][:CHUNK, :] for h in heads]
        for h in heads:
            u_pad = jnp.concatenate([zeros] * c + [u[h]] + [zeros] * (GDN_GROUP - 1 - c), axis=0)
            o_scr[rows, h * GDN_D:(h + 1) * GDN_D] = ws_qs[h][CHUNK:, :] + _mm(attn_bd[h][rows, :], u_pad)
        state = [state[h] * dtot_c[c * CHUNK:c * CHUNK + 1, h:h + 1] + _mm_tn(kd[h][rows, :], u[h])
                 for h in heads]
    for h in heads:
        s_scr[h] = state[h]

    o = o_scr[...]
    outs = []
    for h in heads:
        sl = slice(h * GDN_D, (h + 1) * GDN_D)
        outs.append(_rms(o[:, sl], nw_ref[...]) * _silu(za[:, sl]))
    st_ref[0] = s_scr[...]
    return jnp.concatenate(outs, axis=1).astype(BF16)


def _ssd_pre(proj, cw_ref, cb_ref, pc_ref, cv_ref, xe_scr):
    y, new_hist = _causal_conv4(proj[:, SSD_COL_XBC:SSD_COL_XBC + SSM_XBC], xe_scr, cw_ref)
    cv_ref[0] = new_hist
    xbc = _silu(y + cb_ref[...])
    head_lane = _iota((1, LANES), 1) < SSM_HEADS
    dt_c = jnp.where(head_lane, _softplus(proj[:, SSD_COL_DT:SSD_COL_DT + LANES] + pc_ref[1:2, :]), 0.0)
    return xbc, dt_c


def _ssd_main(pre, zb, pc_ref, d_ref, nw_ref, emat_ref, st_ref, st_scr, y_scr):
    xbc, dt_c = pre
    xs = xbc[:, :SSM_W]
    bm = xbc[:, SSM_W:SSM_W + SSM_BC]
    cm = xbc[:, SSM_W + SSM_BC:]
    head_lane = _iota((1, LANES), 1) < SSM_HEADS
    gam_c = _chunk_cumsum(dt_c * (-jnp.exp(pc_ref[0:1, :])))
    glast_c = _chunk_last(gam_c)
    emat = emat_ref[...]
    e1 = _expand_heads(jnp.where(head_lane, jnp.exp(gam_c), 0.0), emat)
    e2 = _expand_heads(jnp.exp(glast_c - gam_c) * dt_c, emat)
    gam_r = gam_c.T
    dt_r = dt_c.T

    t_i = _iota((CHUNK, CHUNK), 0)
    s_i = _iota((CHUNK, CHUNK), 1)
    incl = t_i >= s_i
    low_half = _iota((1, LANES), 1) < SSM_P

    for c in range(PROMPT_T // CHUNK):
        rows = slice(c * CHUNK, (c + 1) * CHUNK)
        xs_c = xs[rows, :]
        bm_c = bm[rows, :]
        cm_c = cm[rows, :]
        gam_cc = gam_c[rows, :]
        cbs = [_mm_nt(cm_c[:, g * SSM_N:(g + 1) * SSM_N], bm_c[:, g * SSM_N:(g + 1) * SSM_N])
               for g in range(SSM_GROUPS)]
        pairs = []
        for j in range(SSM_HEADS // 2):
            xp = xs_c[:, j * LANES:(j + 1) * LANES]
            acc = None
            for half in range(2):
                h = 2 * j + half
                m = (cbs[h // SSM_RPG]
                     * jnp.exp(jnp.where(incl, gam_cc[:, h:h + 1] - gam_r[h:h + 1, rows], NEG))
                     * dt_r[h:h + 1, rows])
                xm = jnp.where(low_half if half == 0 else jnp.logical_not(low_half), xp, 0.0)
                part = _mm(m, xm)
                acc = part if acc is None else acc + part
            pairs.append(acc)
        y_intra = jnp.concatenate(pairs, axis=1)
        y_inter = jnp.concatenate(
            [_mm(cm_c[:, g * SSM_N:(g + 1) * SSM_N], st_scr[g]) for g in range(SSM_GROUPS)], axis=1)
        y_scr[rows, :] = y_intra + y_inter * e1[rows, :] + d_ref[...] * xs_c
        xe = xs_c * e2[rows, :]
        dtot = e1[(c + 1) * CHUNK - 1:(c + 1) * CHUNK, :]
        for g in range(SSM_GROUPS):
            sl = slice(g * SSM_GW, (g + 1) * SSM_GW)
            st_scr[g] = st_scr[g] * dtot[:, sl] + _mm_tn(bm_c[:, g * SSM_N:(g + 1) * SSM_N], xe[:, sl])

    @pl.when(pl.program_id(1) == pl.num_programs(1) - 1)
    def _():
        for g in range(SSM_GROUPS):
            st_ref[0, g * SSM_RPG:(g + 1) * SSM_RPG] = st_scr[g].T.reshape(SSM_RPG, SSM_P, SSM_N)

    return _rms(y_scr[...] * _silu(zb), nw_ref[...]).astype(BF16)


def _layernorm_silu_gate(u, gate, lnw, lnb):
    mu = jnp.mean(u, axis=-1, keepdims=True)
    uc = u - mu
    var = jnp.mean(uc * uc, axis=-1, keepdims=True)
    return _silu(uc * lax.rsqrt(var + EPS) * lnw + lnb) * _silu(gate)


def _cfm_stage(proj, cw_ref, cb_ref, lnw_ref, lnb_ref, cv_ref, hist_scr):
    glu = proj[:, CFM_COL_GLU:CFM_COL_GLU + 2 * CFM_W]
    u0 = glu[:, :CFM_W] * _sigmoid(glu[:, CFM_W:])
    xe = jnp.concatenate([hist_scr[...], u0], axis=0)
    rolled = [xe] + [pltpu.roll(xe, b, axis=0) for b in range(1, 8)]
    new_hist = u0[PROMPT_T - CFM_HIST:, :]
    hist_scr[...] = new_hist
    cv_ref[0] = new_hist
    gate = proj[:, CFM_COL_GC:CFM_COL_GC + CFM_W]
    outs = []
    for r0 in range(0, PROMPT_T, CFM_ROWS):
        acc = u0[r0:r0 + CFM_ROWS, :] * cw_ref[CFM_K - 1:CFM_K, :]
        for s in range(1, CFM_K):
            a, b = divmod(s, 8)
            lo = CFM_HIST - 8 * a + r0
            acc = acc + rolled[b][lo:lo + CFM_ROWS, :] * cw_ref[CFM_K - 1 - s:CFM_K - s, :]
        outs.append(_layernorm_silu_gate(acc + cb_ref[...], gate[r0:r0 + CFM_ROWS, :], lnw_ref[...],
                                         lnb_ref[...]).astype(BF16))
    return jnp.concatenate(outs, axis=0)


def _merge_stage(x, gates, oa, yb, uc, wa_ref, wb_ref, wc_ref, wo_ref, fw_ref, final):
    dot = functools.partial(jnp.dot, preferred_element_type=F32)
    merged = (_sigmoid(gates[:, :D_MODEL]) * dot(oa, wa_ref[...])
              + _sigmoid(gates[:, D_MODEL:2 * D_MODEL]) * dot(yb, wb_ref[...])
              + _sigmoid(gates[:, 2 * D_MODEL:]) * dot(uc, wc_ref[...]))
    out = x + dot(merged.astype(BF16), wo_ref[...])
    return _rms(out, fw_ref[...]) if final else out


def _prompt_layer_kernel(x_ref, nwin_ref, wg_ref, ws_ref, wc_ref, wgate_ref, wa_ref, wb_ref, wcc_ref, wo_ref, fw_ref,
                         g_cw, g_pc, g_nw, s_cw, s_cb, s_pc, s_d, s_nw, c_cw, c_cb, c_lnw, c_lnb,
                         bd_ref, km_ref, emat_ref,
                         o_ref, gst_ref, gcv_ref, sst_ref, scv_ref, ccv_ref,
                         s_scr, gxe_scr, o_scr, st_scr, sxe_scr, y_scr, chist_scr, *, final):
    @pl.when(pl.program_id(1) == 0)
    def _():
        s_scr[...] = jnp.zeros_like(s_scr)
        st_scr[...] = jnp.zeros_like(st_scr)
        chist_scr[...] = jnp.zeros_like(chist_scr)
        gxe_scr[0:CONV_HIST, :] = jnp.zeros((CONV_HIST, GDN_CH), F32)
        sxe_scr[0:CONV_HIST, :] = jnp.zeros((CONV_HIST, SSM_XBC), F32)

    dot = functools.partial(jnp.dot, preferred_element_type=F32)
    x = x_ref[...]
    h = _rms(x, nwin_ref[...]).astype(BF16)
    pc = dot(h, wc_ref[...])
    pg = dot(h, wg_ref[...])
    uc = _cfm_stage(pc, c_cw, c_cb, c_lnw, c_lnb, ccv_ref, chist_scr)
    ps = dot(h, ws_ref[...])
    gdn_pre = _gdn_pre(pg, g_cw, g_pc, gcv_ref, gxe_scr)
    pgate = dot(h, wgate_ref[...])
    ssd_pre = _ssd_pre(ps, s_cw, s_cb, s_pc, scv_ref, sxe_scr)
    oa = _gdn_main(gdn_pre, pg[:, GDN_COL_ZA:GDN_COL_ZA + GDN_QK_W], g_nw, bd_ref, km_ref, gst_ref, s_scr, o_scr)
    yb = _ssd_main(ssd_pre, ps[:, SSD_COL_ZB:SSD_COL_ZB + SSM_W], s_pc, s_d, s_nw, emat_ref, sst_ref, st_scr,
                   y_scr)
    o_ref[...] = _merge_stage(x, pgate, oa, yb, uc, wa_ref, wb_ref, wcc_ref, wo_ref, fw_ref, final)


def _prompt_layer(x, norm_w, weights, layer, final_w, gdn_p, ssd_p, cfm_p, batch, seq, final):
    nt = seq // PROMPT_T
    tok = lambda width: pl.BlockSpec((PROMPT_T, width), lambda b, t: (b * nt + t, 0))
    const = lambda a: pl.BlockSpec(a.shape, lambda b, t: (0,) * a.ndim)
    resident = lambda a: pl.BlockSpec((None,) + a.shape[1:], lambda b, t: (layer,) + (0,) * (a.ndim - 1),
                                      pipeline_mode=pl.Buffered(1))
    per_seq = lambda *shape: pl.BlockSpec((1,) + shape, lambda b, t: (b,) + (0,) * len(shape))
    chunk_of = jnp.arange(GDN_CAT) // CHUNK
    bd_mask = (chunk_of[:, None] == chunk_of[None, :]).astype(BF16)
    k_mask = (chunk_of[:, None] == (jnp.arange(GDN_GROUP * GDN_D) // GDN_D)[None, :]).astype(BF16)
    small = [final_w] + list(gdn_p) + list(ssd_p) + list(cfm_p) + [bd_mask, k_mask, _expand_matrix()]
    inputs = [x, norm_w] + list(weights) + small
    in_specs = ([tok(D_MODEL), const(norm_w)] + [resident(w) for w in weights] + [const(a) for a in small])
    return pl.pallas_call(
        functools.partial(_prompt_layer_kernel, final=final),
        grid=(batch, nt),
        in_specs=in_specs,
        out_specs=[tok(D_MODEL),
                   per_seq(GDN_HEADS, GDN_D, GDN_D), per_seq(CONV_HIST, GDN_CH),
                   per_seq(SSM_HEADS, SSM_P, SSM_N), per_seq(CONV_HIST, SSM_XBC),
                   per_seq(CFM_HIST, CFM_W)],
        out_shape=[_sds((batch * seq, D_MODEL), F32),
                   _sds((batch, GDN_HEADS, GDN_D, GDN_D), F32), _sds((batch, CONV_HIST, GDN_CH), F32),
                   _sds((batch, SSM_HEADS, SSM_P, SSM_N), F32), _sds((batch, CONV_HIST, SSM_XBC), F32),
                   _sds((batch, CFM_HIST, CFM_W), F32)],
        scratch_shapes=[pltpu.VMEM((GDN_HEADS, GDN_D, GDN_D), F32),
                        pltpu.VMEM((CONV_HIST + PROMPT_T, GDN_CH), F32),
                        pltpu.VMEM((PROMPT_T, GDN_QK_W), F32),
                        pltpu.VMEM((SSM_GROUPS, SSM_N, SSM_GW), F32),
                        pltpu.VMEM((CONV_HIST + PROMPT_T, SSM_XBC), F32),
                        pltpu.VMEM((PROMPT_T, SSM_W), F32),
                        pltpu.VMEM((CFM_HIST, CFM_W), F32)],
        compiler_params=_params(2),
        name="prompt_layer",
    )(*inputs)


def _out_kernel(x_ref, nwin_ref, wg_ref, oa_ref, yb_ref, uc_ref, wa_ref, wb_ref, wc_ref, wo_ref, fw_ref,
                o_ref, *, final):
    x = x_ref[...]
    gates = jnp.dot(_rms(x, nwin_ref[...]).astype(BF16), wg_ref[...], preferred_element_type=F32)
    o_ref[...] = _merge_stage(x, gates, oa_ref[...], yb_ref[...], uc_ref[...],
                              wa_ref, wb_ref, wc_ref, wo_ref, fw_ref, final)


def _out_proj(x, norm_w, w_gate, layer, oa, yb, uc, wa, wb, wc, wo, final_w, final):
    m = x.shape[0]
    const = lambda shape: pl.BlockSpec(shape, lambda i: (0,) * len(shape))
    lw = lambda shape: pl.BlockSpec((None,) + shape, lambda i: (layer,) + (0,) * len(shape))
    return pl.pallas_call(
        functools.partial(_out_kernel, final=final),
        grid=(1,),
        in_specs=[const((m, D_MODEL)), const((1, D_MODEL)), lw((D_MODEL, GATE_PROJ_W)),
                  const((m, GDN_QK_W)), const((m, SSM_W)), const((m, CFM_W)),
                  lw((GDN_QK_W, D_MODEL)), lw((SSM_W, D_MODEL)), lw((CFM_W, D_MODEL)),
                  lw((D_MODEL, D_MODEL)), const((1, D_MODEL))],
        out_specs=const((m, D_MODEL)),
        out_shape=_sds((m, D_MODEL), F32),
        compiler_params=_params(1),
        name="merge_out",
    )(x, norm_w, w_gate, oa, yb, uc, wa, wb, wc, wo, final_w)


SAMPLE_TN = 384


def _proj_kernel(x_ref, nwin_ref, w_ref, o_ref):
    o_ref[...] = _norm_proj(x_ref, nwin_ref, w_ref)


def _sample_proj(x, norm_w, w, layer):
    m = x.shape[0]
    width = w.shape[-1]
    return pl.pallas_call(
        _proj_kernel,
        grid=(width // SAMPLE_TN,),
        in_specs=[pl.BlockSpec((m, D_MODEL), lambda j: (0, 0)),
                  pl.BlockSpec((1, D_MODEL), lambda j: (0, 0)),
                  pl.BlockSpec((None, D_MODEL, SAMPLE_TN), lambda j: (layer, 0, j))],
        out_specs=pl.BlockSpec((m, SAMPLE_TN), lambda j: (0, j)),
        out_shape=_sds((m, width), F32),
        compiler_params=_params(1),
        name="sample_proj",
    )(x, norm_w, w)


TOK_GROUP = 8


def _gdn_sample_kernel(p_ref, buf_ref, cw_ref, pc_ref, nw_ref, s_ref,
                       o_ref, so_ref, bo_ref,
                       kt_scr, qt_scr, v_scr, a_scr, b_scr, qk_scr, o_scr):
    tg = pl.program_id(0)

    @pl.when(tg == 0)
    def _():
        x = p_ref[:, GDN_COL_QKV:GDN_COL_QKV + GDN_CH]
        y = (cw_ref[0:1, :] * buf_ref[0] + cw_ref[1:2, :] * buf_ref[1]
             + cw_ref[2:3, :] * buf_ref[2] + cw_ref[3:4, :] * x)
        bo_ref[0] = buf_ref[1]
        bo_ref[1] = buf_ref[2]
        bo_ref[2] = x
        qkv = _silu(y)
        beta = _sigmoid(p_ref[:, GDN_COL_BA:GDN_COL_BA + LANES])
        decay = jnp.exp(-jnp.exp(pc_ref[0:1, :])
                        * _softplus(p_ref[:, GDN_COL_AA:GDN_COL_AA + LANES] + pc_ref[1:2, :]))
        n_tok = x.shape[0]
        for h in range(GDN_HEADS):
            q = qkv[:, h * GDN_D:(h + 1) * GDN_D]
            k = qkv[:, GDN_QK_W + h * GDN_D:GDN_QK_W + (h + 1) * GDN_D]
            q = q * lax.rsqrt(jnp.sum(q * q, axis=-1, keepdims=True) + EPS) * (GDN_D ** -0.5)
            k = k * lax.rsqrt(jnp.sum(k * k, axis=-1, keepdims=True) + EPS)
            kt_scr[h] = k.T
            qt_scr[h] = q.T
            a_scr[h] = jnp.broadcast_to(decay[:, h:h + 1], (n_tok, LANES))
            b_scr[h] = jnp.broadcast_to(beta[:, h:h + 1], (n_tok, LANES))
            qk_scr[h] = jnp.broadcast_to(jnp.sum(q * k, axis=-1, keepdims=True), (n_tok, LANES))
        v_scr[...] = qkv[:, 2 * GDN_QK_W:]

    shift = (LANES - TOK_GROUP * tg) & (LANES - 1)
    rows = pl.ds(pl.multiple_of(tg * TOK_GROUP, TOK_GROUP), TOK_GROUP)
    for h in range(GDN_HEADS):
        ktg = pltpu.roll(kt_scr[h], shift, axis=1)
        qtg = pltpu.roll(qt_scr[h], shift, axis=1)
        a_blk = a_scr[h, rows, :]
        b_blk = b_scr[h, rows, :]
        qk_blk = qk_scr[h, rows, :]
        v_blk = v_scr[rows, h * GDN_D:(h + 1) * GDN_D]
        o_rows = []
        for j in range(TOK_GROUP):
            s = s_ref[j, h]
            kcol = ktg[:, j:j + 1]
            qcol = qtg[:, j:j + 1]
            ks = jnp.sum(s * kcol, axis=0, keepdims=True)
            qs = jnp.sum(s * qcol, axis=0, keepdims=True)
            a_row = a_blk[j:j + 1, :]
            delta = b_blk[j:j + 1, :] * (v_blk[j:j + 1, :] - a_row * ks)
            so_ref[j, h] = a_row * s + kcol * delta
            o_rows.append(a_row * qs + qk_blk[j:j + 1, :] * delta)
        o_scr[rows, h * GDN_D:(h + 1) * GDN_D] = jnp.concatenate(o_rows, axis=0)

    @pl.when(tg == pl.num_programs(0) - 1)
    def _():
        o = o_scr[...]
        za = p_ref[:, GDN_COL_ZA:GDN_COL_ZA + GDN_QK_W]
        outs = []
        for h in range(GDN_HEADS):
            sl = slice(h * GDN_D, (h + 1) * GDN_D)
            outs.append(_rms(o[:, sl], nw_ref[...]) * _silu(za[:, sl]))
        o_ref[...] = jnp.concatenate(outs, axis=1).astype(BF16)


def _in_place_state(kernel_fn, inputs, in_specs, prev_out, out_index):
    if prev_out is None:
        return kernel_fn, inputs, in_specs, {}
    n = len(inputs)
    wrapped = lambda *refs: kernel_fn(*refs[:n], *refs[n + 1:])
    return wrapped, inputs + [prev_out], in_specs + [pl.BlockSpec(memory_space=pl.ANY)], {n: out_index}


def _gdn_sample(proj, buf_t, conv_w, pc, norm_w, state_all, layer, prev_out):
    n_tok = proj.shape[0]
    const = lambda shape: pl.BlockSpec(shape, lambda g: (0,) * len(shape))
    st_spec = pl.BlockSpec((None, TOK_GROUP, GDN_HEADS, GDN_D, GDN_D), lambda g: (layer, g, 0, 0, 0))
    in_specs = [const((n_tok, GDN_PROJ_W)), const((3, n_tok, GDN_CH)), const((4, GDN_CH)),
                const((8, LANES)), const((1, GDN_D)), st_spec]
    kern, inputs, in_specs, aliases = _in_place_state(
        _gdn_sample_kernel, [proj, buf_t, conv_w, pc, norm_w, state_all], in_specs, prev_out, 1)
    return pl.pallas_call(
        kern,
        grid=(n_tok // TOK_GROUP,),
        in_specs=in_specs,
        out_specs=[const((n_tok, GDN_QK_W)), st_spec, const((3, n_tok, GDN_CH))],
        out_shape=[_sds((n_tok, GDN_QK_W), BF16), _sds(state_all.shape, F32), _sds((3, n_tok, GDN_CH), F32)],
        input_output_aliases=aliases,
        scratch_shapes=[pltpu.VMEM((GDN_HEADS, GDN_D, n_tok), F32),
                        pltpu.VMEM((GDN_HEADS, GDN_D, n_tok), F32),
                        pltpu.VMEM((n_tok, GDN_QK_W), F32),
                        pltpu.VMEM((GDN_HEADS, n_tok, LANES), F32),
                        pltpu.VMEM((GDN_HEADS, n_tok, LANES), F32),
                        pltpu.VMEM((GDN_HEADS, n_tok, LANES), F32),
                        pltpu.VMEM((n_tok, GDN_QK_W), F32)],
        compiler_params=_params(1),
        name="gdn_sample",
    )(*inputs)


def _ssd_sample_kernel(p_ref, buf_ref, cw_ref, cb_ref, pc_ref, d_ref, nw_ref, s_ref,
                       o_ref, so_ref, bo_ref,
                       xt_scr, xs_scr, bm_scr, cm_scr, a_scr, yt_scr):
    tg = pl.program_id(0)
    n_tok = p_ref.shape[0]

    @pl.when(tg == 0)
    def _():
        x = p_ref[:, SSD_COL_XBC:SSD_COL_XBC + SSM_XBC]
        y = (cw_ref[0:1, :] * buf_ref[0] + cw_ref[1:2, :] * buf_ref[1]
             + cw_ref[2:3, :] * buf_ref[2] + cw_ref[3:4, :] * x)
        bo_ref[0] = buf_ref[1]
        bo_ref[1] = buf_ref[2]
        bo_ref[2] = x
        xbc = _silu(y + cb_ref[...])
        xs = xbc[:, :SSM_W]
        xs_scr[...] = xs
        bm_scr[...] = xbc[:, SSM_W:SSM_W + SSM_BC]
        cm_scr[...] = xbc[:, SSM_W + SSM_BC:]
        head_lane = _iota((1, LANES), 1) < SSM_HEADS
        dt = jnp.where(head_lane, _softplus(p_ref[:, SSD_COL_DT:SSD_COL_DT + LANES] + pc_ref[1:2, :]), 0.0)
        decay = jnp.exp(dt * (-jnp.exp(pc_ref[0:1, :])))
        xdt_t = (xs * _expand_heads(dt, _expand_matrix())).T
        hi = xdt_t.astype(BF16)
        xt_scr[:, :n_tok] = hi
        xt_scr[:, n_tok:] = (xdt_t - hi.astype(F32)).astype(BF16)
        for h in range(SSM_HEADS):
            a_scr[h] = jnp.broadcast_to(decay[:, h:h + 1], (n_tok, LANES))
        yt_scr[...] = jnp.zeros_like(yt_scr)

    rows = pl.ds(pl.multiple_of(tg * TOK_GROUP, TOK_GROUP), TOK_GROUP)
    bm_blk = bm_scr[rows, :]
    cm_blk = cm_scr[rows, :]
    lane = _iota((1, LANES), 1)
    piece_tok = _iota((2 * n_tok, LANES), 0) & (n_tok - 1)
    y_tile = jnp.zeros((SSM_W, LANES), F32)
    for j in range(TOK_GROUP):
        pick = jnp.where(piece_tok == tg * TOK_GROUP + j, 1.0, 0.0).astype(BF16)
        xb = jnp.dot(xt_scr[...], pick, preferred_element_type=F32)
        prods = []
        for h in range(SSM_HEADS):
            g = h // SSM_RPG
            s = s_ref[j, h]
            a_row = a_scr[h, rows, :][j:j + 1, :]
            xcol = xb[h * SSM_P:(h + 1) * SSM_P, :]
            s_new = a_row * s + xcol * bm_blk[j:j + 1, g * SSM_N:(g + 1) * SSM_N]
            so_ref[j, h] = s_new
            prods.append(s_new * cm_blk[j:j + 1, g * SSM_N:(g + 1) * SSM_N])
        y_col = jnp.sum(jnp.concatenate(prods, axis=0), axis=1, keepdims=True)
        y_tile = jnp.where(lane == j, y_col, y_tile)
    in_group = (lane >> 3) == tg
    yt_scr[...] = jnp.where(in_group, pltpu.roll(y_tile, TOK_GROUP * tg, axis=1), yt_scr[...])

    @pl.when(tg == pl.num_programs(0) - 1)
    def _():
        y = yt_scr[...].T + d_ref[...] * xs_scr[...]
        zb = p_ref[:, SSD_COL_ZB:SSD_COL_ZB + SSM_W]
        o_ref[...] = _rms(y * _silu(zb), nw_ref[...]).astype(BF16)


def _ssd_sample(proj, buf_t, conv_w, conv_b, pc, d_row, norm_w, state_all, layer, prev_out):
    n_tok = proj.shape[0]
    const = lambda shape: pl.BlockSpec(shape, lambda g: (0,) * len(shape))
    st_spec = pl.BlockSpec((None, TOK_GROUP, SSM_HEADS, SSM_P, SSM_N), lambda g: (layer, g, 0, 0, 0))
    in_specs = [const((n_tok, SSD_PROJ_W)), const((3, n_tok, SSM_XBC)), const((4, SSM_XBC)), const((1, SSM_XBC)),
                const((8, LANES)), const((1, SSM_W)), const((1, SSM_W)), st_spec]
    kern, inputs, in_specs, aliases = _in_place_state(
        _ssd_sample_kernel, [proj, buf_t, conv_w, conv_b, pc, d_row, norm_w, state_all], in_specs, prev_out, 1)
    return pl.pallas_call(
        kern,
        grid=(n_tok // TOK_GROUP,),
        in_specs=in_specs,
        out_specs=[const((n_tok, SSM_W)), st_spec, const((3, n_tok, SSM_XBC))],
        out_shape=[_sds((n_tok, SSM_W), BF16), _sds(state_all.shape, F32), _sds((3, n_tok, SSM_XBC), F32)],
        input_output_aliases=aliases,
        scratch_shapes=[pltpu.VMEM((SSM_W, 2 * n_tok), BF16),
                        pltpu.VMEM((n_tok, SSM_W), F32),
                        pltpu.VMEM((n_tok, SSM_BC), F32),
                        pltpu.VMEM((n_tok, SSM_BC), F32),
                        pltpu.VMEM((SSM_HEADS, n_tok, LANES), F32),
                        pltpu.VMEM((SSM_W, n_tok), F32)],
        compiler_params=_params(1),
        name="ssd_sample",
    )(*inputs)


def _cfm_sample_kernel(p_ref, buf_ref, cw_ref, cb_ref, lnw_ref, lnb_ref, o_ref, bo_ref):
    glu = p_ref[:, CFM_COL_GLU:CFM_COL_GLU + 2 * CFM_W]
    u0 = glu[:, :CFM_W] * _sigmoid(glu[:, CFM_W:])
    acc = cw_ref[CFM_K - 1:CFM_K, :] * u0 + cb_ref[...]
    for j in range(CFM_K - 1):
        row = buf_ref[j]
        acc = acc + cw_ref[j:j + 1, :] * row
        if j > 0:
            bo_ref[j - 1] = row
    bo_ref[CFM_K - 2] = u0
    gate = p_ref[:, CFM_COL_GC:CFM_COL_GC + CFM_W]
    o_ref[...] = _layernorm_silu_gate(acc, gate, lnw_ref[...], lnb_ref[...]).astype(BF16)


def _cfm_sample(proj, buf_t, conv_w, conv_b, ln_w, ln_b):
    n_tok = proj.shape[0]
    const = lambda shape: pl.BlockSpec(shape, lambda i: (0,) * len(shape))
    return pl.pallas_call(
        _cfm_sample_kernel,
        grid=(1,),
        in_specs=[const((n_tok, CFM_PROJ_W)), const((CFM_K - 1, n_tok, CFM_W)), const((CFM_K, CFM_W)),
                  const((1, CFM_W)), const((1, CFM_W)), const((1, CFM_W))],
        out_specs=[const((n_tok, CFM_W)), const((CFM_K - 1, n_tok, CFM_W))],
        out_shape=[_sds((n_tok, CFM_W), BF16), _sds((CFM_K - 1, n_tok, CFM_W), F32)],
        compiler_params=_params(1),
        name="cfm_sample",
    )(proj, buf_t, conv_w, conv_b, ln_w, ln_b)


_IN_Q, _IN_ZA, _IN_BA, _IN_AA, _IN_ZB, _IN_XBC, _IN_DT, _IN_GLU, _IN_GC, _IN_GATES, _IN_END = (
    0, 1536, 2048, 2052, 2056, 3080, 4616, 4632, 5656, 6168, 9240)


def _prep_w_in(w_in):
    seg = lambda a, b: w_in[:, :, a:b]
    zeros = lambda n: jnp.zeros(w_in.shape[:2] + (n,), w_in.dtype)
    w_gdn = jnp.concatenate([seg(_IN_Q, _IN_ZA), seg(_IN_ZA, _IN_BA),
                             seg(_IN_BA, _IN_AA), zeros(LANES - GDN_HEADS),
                             seg(_IN_AA, _IN_ZB), zeros(LANES - GDN_HEADS)], axis=-1)
    w_ssd = jnp.concatenate([seg(_IN_XBC, _IN_DT), seg(_IN_ZB, _IN_XBC),
                             seg(_IN_DT, _IN_GLU), zeros(LANES - SSM_HEADS)], axis=-1)
    w_cfm = seg(_IN_GLU, _IN_GATES)
    w_gate = seg(_IN_GATES, _IN_END)
    return tuple(w.astype(BF16) for w in (w_gdn, w_ssd, w_cfm, w_gate))


def _lane_pad(v, rows=8):
    out = jnp.zeros((v[0].shape[0], rows, LANES), F32)
    for i, a in enumerate(v):
        out = out.at[:, i, :a.shape[1]].set(a)
    return out


def kernel(x_prompt, x_sample, state_gdn, state_gdn_conv, state_ssm, state_ssm_conv, state_cfm_conv,
           norm_w, w_in, gdn_conv_w, gdn_a_log, gdn_dt_bias, gdn_norm_w, gdn_w_o,
           ssm_conv_w, ssm_conv_b, ssm_a_log, ssm_dt_bias, ssm_d, ssm_norm_w, ssm_w_o,
           cfm_conv_w, cfm_conv_b, cfm_ln_w, cfm_ln_b, cfm_w_o, w_out, final_norm_w):
    depth = w_in.shape[0]
    batch, seq, _ = x_prompt.shape
    n_tok = x_sample.shape[0]
    assert n_tok == LANES and seq % PROMPT_T == 0

    w_gdn, w_ssd, w_cfm, w_gate = _prep_w_in(w_in)
    wa, wb, wc, wo = (w.astype(BF16) for w in (gdn_w_o, ssm_w_o, cfm_w_o, w_out))
    gdn_pc = _lane_pad([gdn_a_log, gdn_dt_bias])
    ssm_pc = _lane_pad([ssm_a_log, ssm_dt_bias])
    ssm_d_row = jnp.repeat(ssm_d, SSM_P, axis=1)[:, None, :]
    final_w = final_norm_w[None, :]

    hp = x_prompt.reshape(batch * seq, D_MODEL)
    hs = x_sample.reshape(n_tok, D_MODEL)
    p_states, s_states = [], []
    new_gdn_s = new_ssm_s = None
    for i in range(depth):
        final = i == depth - 1
        row = lambda a: a[i][None, :]
        nw = row(norm_w)

        gdn_p = (gdn_conv_w[i], gdn_pc[i], row(gdn_norm_w))
        ssd_p = (ssm_conv_w[i], row(ssm_conv_b), ssm_pc[i], ssm_d_row[i], row(ssm_norm_w))
        cfm_p = (cfm_conv_w[i], row(cfm_conv_b), row(cfm_ln_w), row(cfm_ln_b))
        hp, gdn_s, gdn_cv, ssm_s, ssm_cv, cfm_cv = _prompt_layer(
            hp, nw, (w_gdn, w_ssd, w_cfm, w_gate, wa, wb, wc, wo), i, final_w, gdn_p, ssd_p, cfm_p,
            batch, seq, final)
        p_states.append((gdn_s, gdn_cv[:, CONV_HIST - 3:], ssm_s, ssm_cv[:, CONV_HIST - 3:],
                         cfm_cv[:, CFM_HIST - (CFM_K - 1):]))

        oa, new_gdn_s, gdn_buf = _gdn_sample(_sample_proj(hs, nw, w_gdn, i), jnp.swapaxes(state_gdn_conv[i], 0, 1),
                                             gdn_conv_w[i], gdn_pc[i], row(gdn_norm_w), state_gdn, i, new_gdn_s)
        yb, new_ssm_s, ssm_buf = _ssd_sample(_sample_proj(hs, nw, w_ssd, i), jnp.swapaxes(state_ssm_conv[i], 0, 1),
                                             ssm_conv_w[i], row(ssm_conv_b), ssm_pc[i], ssm_d_row[i],
                                             row(ssm_norm_w), state_ssm, i, new_ssm_s)
        uc, cfm_buf = _cfm_sample(_sample_proj(hs, nw, w_cfm, i), jnp.swapaxes(state_cfm_conv[i], 0, 1),
                                  cfm_conv_w[i], row(cfm_conv_b), row(cfm_ln_w), row(cfm_ln_b))
        hs = _out_proj(hs, nw, w_gate, i, oa, yb, uc, wa, wb, wc, wo, final_w, final)
        s_states.append((jnp.swapaxes(gdn_buf, 0, 1), jnp.swapaxes(ssm_buf, 0, 1), jnp.swapaxes(cfm_buf, 0, 1)))

    stack = lambda states, j: jnp.stack([s[j] for s in states])
    return (hp.reshape(batch, seq, D_MODEL), hs.reshape(n_tok, 1, D_MODEL),
            stack(p_states, 0), stack(p_states, 1), stack(p_states, 2), stack(p_states, 3), stack(p_states, 4),
            new_gdn_s, stack(s_states, 0), new_ssm_s, stack(s_states, 1), stack(s_states, 2))
```

```python
import functools

import jax
import jax.numpy as jnp
from jax import lax
from jax.experimental import pallas as pl
from jax.experimental.pallas import tpu as pltpu

F32 = jnp.float32
BF16 = jnp.bfloat16

D_MODEL = 1024
GDN_HEADS = 4
GDN_D = 128
GDN_QK_W = GDN_HEADS * GDN_D
GDN_CH = 3 * GDN_QK_W
SSM_W = 1024
SSM_P = 64
SSM_HEADS = 16
SSM_GROUPS = 2
SSM_RPG = SSM_HEADS // SSM_GROUPS
SSM_GW = SSM_RPG * SSM_P
SSM_N = 128
SSM_BC = SSM_GROUPS * SSM_N
SSM_XBC = SSM_W + 2 * SSM_BC
CFM_W = 512
CFM_K = 31
CFM_HIST = 32
CFM_ROWS = 32
CONV_HIST = 8
CHUNK = 64
CHUNK_SHIFT = 6
EPS = 1e-6
NEG = -1e30
LANES = 128

GDN_COL_QKV, GDN_COL_ZA, GDN_COL_BA, GDN_COL_AA, GDN_PROJ_W = 0, 1536, 2048, 2176, 2304
SSD_COL_XBC, SSD_COL_ZB, SSD_COL_DT, SSD_PROJ_W = 0, 1536, 2560, 2688
CFM_COL_GLU, CFM_COL_GC, CFM_PROJ_W = 0, 1024, 1536
GATE_PROJ_W = 3 * D_MODEL

PROMPT_T = 256
VMEM_LIMIT = 56 * 1024 * 1024


def _sds(shape, dtype):
    return jax.ShapeDtypeStruct(shape, dtype)


def _params(n_axes):
    return pltpu.CompilerParams(dimension_semantics=("arbitrary",) * n_axes,
                                vmem_limit_bytes=VMEM_LIMIT)


def _sigmoid(x):
    return jax.nn.sigmoid(x)


def _silu(x):
    return x * jax.nn.sigmoid(x)


def _softplus(x):
    return jnp.maximum(x, 0.0) + jnp.log1p(jnp.exp(-jnp.abs(x)))


def _mm(a, b):
    return jnp.dot(a.astype(BF16), b.astype(BF16), preferred_element_type=F32)


def _mm_nt(a, b):
    return lax.dot_general(a.astype(BF16), b.astype(BF16), (((1,), (1,)), ((), ())),
                           preferred_element_type=F32)


def _mm_tn(a, b):
    return lax.dot_general(a.astype(BF16), b.astype(BF16), (((0,), (0,)), ((), ())),
                           preferred_element_type=F32)


def _rms(x, w):
    return x * lax.rsqrt(jnp.mean(x * x, axis=-1, keepdims=True) + EPS) * w


def _norm_proj(x_ref, nw_ref, w_ref):
    h = _rms(x_ref[...], nw_ref[...]).astype(BF16)
    return jnp.dot(h, w_ref[...], preferred_element_type=F32)


def _iota(shape, dim):
    return lax.broadcasted_iota(jnp.int32, shape, dim)


def _chunk_cumsum(x):
    pos = _iota(x.shape, 0) & (CHUNK - 1)
    d = 1
    while d < CHUNK:
        x = x + jnp.where(pos >= d, pltpu.roll(x, d, axis=0), 0.0)
        d *= 2
    return x


def _chunk_last(x):
    n = x.shape[0] // CHUNK
    return jnp.concatenate(
        [jnp.broadcast_to(x[(c + 1) * CHUNK - 1:(c + 1) * CHUNK, :], (CHUNK, x.shape[1])) for c in range(n)],
        axis=0)


def _expand_heads(x, emat):
    hi = x.astype(BF16).astype(F32)
    r1 = x - hi
    mid = r1.astype(BF16).astype(F32)
    lo = r1 - mid
    packed = hi + pltpu.roll(mid, SSM_HEADS, axis=1) + pltpu.roll(lo, 2 * SSM_HEADS, axis=1)
    return jnp.dot(packed.astype(BF16), emat, preferred_element_type=F32)


def _expand_matrix():
    r = _iota((LANES, SSM_W), 0)
    c = _iota((LANES, SSM_W), 1)
    return jnp.where(((r & (SSM_HEADS - 1)) == (c >> CHUNK_SHIFT)) & (r < 3 * SSM_HEADS), 1.0, 0.0).astype(BF16)


def _causal_conv4(x, xe_scr, w_ref):
    t_len = x.shape[0]
    xe = jnp.concatenate([xe_scr[0:CONV_HIST, :], x], axis=0)
    acc = xe * w_ref[0:1, :]
    for j in range(1, 4):
        acc = pltpu.roll(acc, 1, axis=0) + xe * w_ref[j:j + 1, :]
    new_hist = x[t_len - CONV_HIST:, :]
    xe_scr[0:CONV_HIST, :] = new_hist
    return acc[CONV_HIST:, :], new_hist


GDN_GROUP = PROMPT_T // CHUNK
GDN_CAT = GDN_GROUP * CHUNK


def _lane_cat(x):
    return jnp.concatenate([x[c * CHUNK:(c + 1) * CHUNK, :] for c in range(GDN_GROUP)], axis=1)


def _gdn_pre(proj, cw_ref, pc_ref, cv_ref, xe_scr):
    y, new_hist = _causal_conv4(proj[:, GDN_COL_QKV:GDN_COL_QKV + GDN_CH], xe_scr, cw_ref)
    cv_ref[0] = new_hist
    qkv = _silu(y)
    q_n, k_n, v_n = [], [], []
    for h in range(GDN_HEADS):
        q = qkv[:, h * GDN_D:(h + 1) * GDN_D]
        k = qkv[:, GDN_QK_W + h * GDN_D:GDN_QK_W + (h + 1) * GDN_D]
        q_n.append(q * lax.rsqrt(jnp.sum(q * q, axis=-1, keepdims=True) + EPS) * (GDN_D ** -0.5))
        k_n.append(k * lax.rsqrt(jnp.sum(k * k, axis=-1, keepdims=True) + EPS))
        v_n.append(qkv[:, 2 * GDN_QK_W + h * GDN_D:2 * GDN_QK_W + (h + 1) * GDN_D])
    beta_c = _sigmoid(proj[:, GDN_COL_BA:GDN_COL_BA + LANES])
    g_c = -jnp.exp(pc_ref[0:1, :]) * _softplus(proj[:, GDN_COL_AA:GDN_COL_AA + LANES] + pc_ref[1:2, :])
    return q_n, k_n, v_n, beta_c, g_c


def _gdn_main(pre, za, nw_ref, bd_ref, km_ref, st_ref, s_scr, o_scr):
    q_n, k_n, v_n, beta_c, g_c = pre
    heads = range(GDN_HEADS)
    gam_c = _chunk_cumsum(g_c)
    glast_c = _chunk_last(gam_c)
    eg_c = jnp.exp(gam_c)
    ekd_c = jnp.exp(glast_c - gam_c)
    dtot_c = jnp.exp(glast_c)
    bg_c = beta_c * eg_c
    gam_r = gam_c.T
    col = lambda a, h: a[:, h:h + 1]

    t_i = _iota((CHUNK, GDN_CAT), 0)
    s_i = _iota((CHUNK, GDN_CAT), 1) & (CHUNK - 1)
    incl, strict = t_i >= s_i, t_i > s_i
    eye_cat = jnp.where(t_i == s_i, 1.0, 0.0)
    bd_mask = bd_ref[...]
    k_mask = km_ref[...]
    low_half = _iota((1, LANES), 1) < CHUNK

    def block_diag(y_cat):
        return jnp.concatenate([y_cat.astype(BF16)] * GDN_GROUP, axis=0) * bd_mask

    kb = [k_n[h] * col(beta_c, h) for h in heads]
    aq = []
    for h in heads:
        rhs_nt = jnp.concatenate([k_n[h].astype(BF16)] * GDN_GROUP, axis=1) * k_mask
        aq.append(_mm_nt(jnp.concatenate([_lane_cat(kb[h]), _lane_cat(q_n[h])], axis=0), rhs_nt))

    dincl = []
    for h in heads:
        gcol = col(gam_c, h)
        tiles = []
        for m in range(GDN_GROUP // 2):
            lo = jnp.broadcast_to(gcol[2 * m * CHUNK:(2 * m + 1) * CHUNK, :], (CHUNK, LANES))
            hi = jnp.broadcast_to(gcol[(2 * m + 1) * CHUNK:(2 * m + 2) * CHUNK, :], (CHUNK, LANES))
            tiles.append(jnp.where(low_half, lo, hi))
        dincl.append(jnp.exp(jnp.where(incl, jnp.concatenate(tiles, axis=1) - gam_r[h:h + 1, :], NEG)))
    n0 = [aq[h][:CHUNK, :] * jnp.where(strict, dincl[h], 0.0) for h in heads]
    attn_bd = [block_diag(aq[h][CHUNK:, :] * dincl[h]) for h in heads]

    def lower_left(m):
        span = 2 * m - 1
        return ((t_i | span) == (s_i | span)) & ((t_i & span) >= m) & ((s_i & span) < m)

    tinv = [eye_cat - jnp.where(lower_left(1), n0[h], 0.0) for h in heads]
    for m in (2, 4, 8, 16, 32):
        ll = lower_left(m)
        b_ainv = [_mm(jnp.where(ll, n0[h], 0.0), block_diag(tinv[h])) for h in heads]
        tinv = [tinv[h] - _mm(tinv[h], block_diag(b_ainv[h])) for h in heads]

    wu = [_mm(block_diag(tinv[h]),
              jnp.concatenate([k_n[h] * col(bg_c, h), v_n[h] * col(beta_c, h)], axis=1)) for h in heads]
    qg = [q_n[h] * col(eg_c, h) for h in heads]
    kd = [k_n[h] * col(ekd_c, h) for h in heads]

    state = [s_scr[h] for h in heads]
    zeros = jnp.zeros((CHUNK, GDN_D), F32)
    for c in range(GDN_GROUP):
        rows = slice(c * CHUNK, (c + 1) * CHUNK)
        ws_qs = [_mm(jnp.concatenate([wu[h][rows, :GDN_D], qg[h][rows, :]], axis=0), state[h]) for h in heads]
        u = [wu[h][rows, GDN_D:] - ws_qs[h][:CHUNK, :] for h in heads]
        for h in heads:
            u_pad = jnp.concatenate([zeros] * c + [u[h]] + [zeros] * (GDN_GROUP - 1 - c), axis=0)
            o_scr[rows, h * GDN_D:(h + 1) * GDN_D] = ws_qs[h][CHUNK:, :] + _mm(attn_bd[h][rows, :], u_pad)
        state = [state[h] * dtot_c[c * CHUNK:c * CHUNK + 1, h:h + 1] + _mm_tn(kd[h][rows, :], u[h])
                 for h in heads]
    for h in heads:
        s_scr[h] = state[h]

    o = o_scr[...]
    outs = []
    for h in heads:
        sl = slice(h * GDN_D, (h + 1) * GDN_D)
        outs.append(_rms(o[:, sl], nw_ref[...]) * _silu(za[:, sl]))
    st_ref[0] = s_scr[...]
    return jnp.concatenate(outs, axis=1).astype(BF16)


def _ssd_pre(proj, cw_ref, cb_ref, pc_ref, cv_ref, xe_scr):
    y, new_hist = _causal_conv4(proj[:, SSD_COL_XBC:SSD_COL_XBC + SSM_XBC], xe_scr, cw_ref)
    cv_ref[0] = new_hist
    xbc = _silu(y + cb_ref[...])
    head_lane = _iota((1, LANES), 1) < SSM_HEADS
    dt_c = jnp.where(head_lane, _softplus(proj[:, SSD_COL_DT:SSD_COL_DT + LANES] + pc_ref[1:2, :]), 0.0)
    return xbc, dt_c


def _ssd_main(pre, zb, pc_ref, d_ref, nw_ref, emat_ref, st_ref, st_scr, y_scr):
    xbc, dt_c = pre
    xs = xbc[:, :SSM_W]
    bm = xbc[:, SSM_W:SSM_W + SSM_BC]
    cm = xbc[:, SSM_W + SSM_BC:]
    head_lane = _iota((1, LANES), 1) < SSM_HEADS
    gam_c = _chunk_cumsum(dt_c * (-jnp.exp(pc_ref[0:1, :])))
    glast_c = _chunk_last(gam_c)
    emat = emat_ref[...]
    e1 = _expand_heads(jnp.where(head_lane, jnp.exp(gam_c), 0.0), emat)
    e2 = _expand_heads(jnp.exp(glast_c - gam_c) * dt_c, emat)
    gam_r = gam_c.T
    dt_r = dt_c.T

    t_i = _iota((CHUNK, CHUNK), 0)
    s_i = _iota((CHUNK, CHUNK), 1)
    incl = t_i >= s_i
    low_half = _iota((1, LANES), 1) < SSM_P

    for c in range(PROMPT_T // CHUNK):
        rows = slice(c * CHUNK, (c + 1) * CHUNK)
        xs_c = xs[rows, :]
        bm_c = bm[rows, :]
        cm_c = cm[rows, :]
        gam_cc = gam_c[rows, :]
        cbs = [_mm_nt(cm_c[:, g * SSM_N:(g + 1) * SSM_N], bm_c[:, g * SSM_N:(g + 1) * SSM_N])
               for g in range(SSM_GROUPS)]
        pairs = []
        for j in range(SSM_HEADS // 2):
            xp = xs_c[:, j * LANES:(j + 1) * LANES]
            acc = None
            for half in range(2):
                h = 2 * j + half
                m = (cbs[h // SSM_RPG]
                     * jnp.exp(jnp.where(incl, gam_cc[:, h:h + 1] - gam_r[h:h + 1, rows], NEG))
                     * dt_r[h:h + 1, rows])
                xm = jnp.where(low_half if half == 0 else jnp.logical_not(low_half), xp, 0.0)
                part = _mm(m, xm)
                acc = part if acc is None else acc + part
            pairs.append(acc)
        y_intra = jnp.concatenate(pairs, axis=1)
        y_inter = jnp.concatenate(
            [_mm(cm_c[:, g * SSM_N:(g + 1) * SSM_N], st_scr[g]) for g in range(SSM_GROUPS)], axis=1)
        y_scr[rows, :] = y_intra + y_inter * e1[rows, :] + d_ref[...] * xs_c
        xe = xs_c * e2[rows, :]
        dtot = e1[(c + 1) * CHUNK - 1:(c + 1) * CHUNK, :]
        for g in range(SSM_GROUPS):
            sl = slice(g * SSM_GW, (g + 1) * SSM_GW)
            st_scr[g] = st_scr[g] * dtot[:, sl] + _mm_tn(bm_c[:, g * SSM_N:(g + 1) * SSM_N], xe[:, sl])

    @pl.when(pl.program_id(1) == pl.num_programs(1) - 1)
    def _():
        for g in range(SSM_GROUPS):
            st_ref[0, g * SSM_RPG:(g + 1) * SSM_RPG] = st_scr[g].T.reshape(SSM_RPG, SSM_P, SSM_N)

    return _rms(y_scr[...] * _silu(zb), nw_ref[...]).astype(BF16)


def _layernorm_silu_gate(u, gate, lnw, lnb):
    mu = jnp.mean(u, axis=-1, keepdims=True)
    uc = u - mu
    var = jnp.mean(uc * uc, axis=-1, keepdims=True)
    return _silu(uc * lax.rsqrt(var + EPS) * lnw + lnb) * _silu(gate)


def _cfm_stage(proj, cw_ref, cb_ref, lnw_ref, lnb_ref, cv_ref, hist_scr):
    glu = proj[:, CFM_COL_GLU:CFM_COL_GLU + 2 * CFM_W]
    u0 = glu[:, :CFM_W] * _sigmoid(glu[:, CFM_W:])
    xe = jnp.concatenate([hist_scr[...], u0], axis=0)
    rolled = [xe] + [pltpu.roll(xe, b, axis=0) for b in range(1, 8)]
    new_hist = u0[PROMPT_T - CFM_HIST:, :]
    hist_scr[...] = new_hist
    cv_ref[0] = new_hist
    gate = proj[:, CFM_COL_GC:CFM_COL_GC + CFM_W]
    outs = []
    for r0 in range(0, PROMPT_T, CFM_ROWS):
        acc = u0[r0:r0 + CFM_ROWS, :] * cw_ref[CFM_K - 1:CFM_K, :]
        for s in range(1, CFM_K):
            a, b = divmod(s, 8)
            lo = CFM_HIST - 8 * a + r0
            acc = acc + rolled[b][lo:lo + CFM_ROWS, :] * cw_ref[CFM_K - 1 - s:CFM_K - s, :]
        outs.append(_layernorm_silu_gate(acc + cb_ref[...], gate[r0:r0 + CFM_ROWS, :], lnw_ref[...],
                                         lnb_ref[...]).astype(BF16))
    return jnp.concatenate(outs, axis=0)


def _merge_stage(x, gates, oa, yb, uc, wa_ref, wb_ref, wc_ref, wo_ref, fw_ref, final):
    dot = functools.partial(jnp.dot, preferred_element_type=F32)
    merged = (_sigmoid(gates[:, :D_MODEL]) * dot(oa, wa_ref[...])
              + _sigmoid(gates[:, D_MODEL:2 * D_MODEL]) * dot(yb, wb_ref[...])
              + _sigmoid(gates[:, 2 * D_MODEL:]) * dot(uc, wc_ref[...]))
    out = x + dot(merged.astype(BF16), wo_ref[...])
    return _rms(out, fw_ref[...]) if final else out


def _prompt_layer_kernel(x_ref, nwin_ref, wg_ref, ws_ref, wc_ref, wgate_ref, wa_ref, wb_ref, wcc_ref, wo_ref, fw_ref,
                         g_cw, g_pc, g_nw, s_cw, s_cb, s_pc, s_d, s_nw, c_cw, c_cb, c_lnw, c_lnb,
                         bd_ref, km_ref, emat_ref,
                         o_ref, gst_ref, gcv_ref, sst_ref, scv_ref, ccv_ref,
                         s_scr, gxe_scr, o_scr, st_scr, sxe_scr, y_scr, chist_scr, *, final):
    @pl.when(pl.program_id(1) == 0)
    def _():
        s_scr[...] = jnp.zeros_like(s_scr)
        st_scr[...] = jnp.zeros_like(st_scr)
        chist_scr[...] = jnp.zeros_like(chist_scr)
        gxe_scr[0:CONV_HIST, :] = jnp.zeros((CONV_HIST, GDN_CH), F32)
        sxe_scr[0:CONV_HIST, :] = jnp.zeros((CONV_HIST, SSM_XBC), F32)

    dot = functools.partial(jnp.dot, preferred_element_type=F32)
    x = x_ref[...]
    h = _rms(x, nwin_ref[...]).astype(BF16)
    pc = dot(h, wc_ref[...])
    pg = dot(h, wg_ref[...])
    uc = _cfm_stage(pc, c_cw, c_cb, c_lnw, c_lnb, ccv_ref, chist_scr)
    ps = dot(h, ws_ref[...])
    gdn_pre = _gdn_pre(pg, g_cw, g_pc, gcv_ref, gxe_scr)
    pgate = dot(h, wgate_ref[...])
    ssd_pre = _ssd_pre(ps, s_cw, s_cb, s_pc, scv_ref, sxe_scr)
    oa = _gdn_main(gdn_pre, pg[:, GDN_COL_ZA:GDN_COL_ZA + GDN_QK_W], g_nw, bd_ref, km_ref, gst_ref, s_scr, o_scr)
    yb = _ssd_main(ssd_pre, ps[:, SSD_COL_ZB:SSD_COL_ZB + SSM_W], s_pc, s_d, s_nw, emat_ref, sst_ref, st_scr,
                   y_scr)
    o_ref[...] = _merge_stage(x, pgate, oa, yb, uc, wa_ref, wb_ref, wcc_ref, wo_ref, fw_ref, final)


def _prompt_layer(x, norm_w, weights, layer, final_w, gdn_p, ssd_p, cfm_p, batch, seq, final):
    nt = seq // PROMPT_T
    tok = lambda width: pl.BlockSpec((PROMPT_T, width), lambda b, t: (b * nt + t, 0))
    const = lambda a: pl.BlockSpec(a.shape, lambda b, t: (0,) * a.ndim)
    resident = lambda a: pl.BlockSpec((None,) + a.shape[1:], lambda b, t: (layer,) + (0,) * (a.ndim - 1),
                                      pipeline_mode=pl.Buffered(1))
    per_seq = lambda *shape: pl.BlockSpec((1,) + shape, lambda b, t: (b,) + (0,) * len(shape))
    chunk_of = jnp.arange(GDN_CAT) // CHUNK
    bd_mask = (chunk_of[:, None] == chunk_of[None, :]).astype(BF16)
    k_mask = (chunk_of[:, None] == (jnp.arange(GDN_GROUP * GDN_D) // GDN_D)[None, :]).astype(BF16)
    small = [final_w] + list(gdn_p) + list(ssd_p) + list(cfm_p) + [bd_mask, k_mask, _expand_matrix()]
    inputs = [x, norm_w] + list(weights) + small
    in_specs = ([tok(D_MODEL), const(norm_w)] + [resident(w) for w in weights] + [const(a) for a in small])
    return pl.pallas_call(
        functools.partial(_prompt_layer_kernel, final=final),
        grid=(batch, nt),
        in_specs=in_specs,
        out_specs=[tok(D_MODEL),
                   per_seq(GDN_HEADS, GDN_D, GDN_D), per_seq(CONV_HIST, GDN_CH),
                   per_seq(SSM_HEADS, SSM_P, SSM_N), per_seq(CONV_HIST, SSM_XBC),
                   per_seq(CFM_HIST, CFM_W)],
        out_shape=[_sds((batch * seq, D_MODEL), F32),
                   _sds((batch, GDN_HEADS, GDN_D, GDN_D), F32), _sds((batch, CONV_HIST, GDN_CH), F32),
                   _sds((batch, SSM_HEADS, SSM_P, SSM_N), F32), _sds((batch, CONV_HIST, SSM_XBC), F32),
                   _sds((batch, CFM_HIST, CFM_W), F32)],
        scratch_shapes=[pltpu.VMEM((GDN_HEADS, GDN_D, GDN_D), F32),
                        pltpu.VMEM((CONV_HIST + PROMPT_T, GDN_CH), F32),
                        pltpu.VMEM((PROMPT_T, GDN_QK_W), F32),
                        pltpu.VMEM((SSM_GROUPS, SSM_N, SSM_GW), F32),
                        pltpu.VMEM((CONV_HIST + PROMPT_T, SSM_XBC), F32),
                        pltpu.VMEM((PROMPT_T, SSM_W), F32),
                        pltpu.VMEM((CFM_HIST, CFM_W), F32)],
        compiler_params=_params(2),
        name="prompt_layer",
    )(*inputs)


def _out_kernel(x_ref, nwin_ref, wg_ref, oa_ref, yb_ref, uc_ref, wa_ref, wb_ref, wc_ref, wo_ref, fw_ref,
                o_ref, *, final):
    x = x_ref[...]
    gates = jnp.dot(_rms(x, nwin_ref[...]).astype(BF16), wg_ref[...], preferred_element_type=F32)
    o_ref[...] = _merge_stage(x, gates, oa_ref[...], yb_ref[...], uc_ref[...],
                              wa_ref, wb_ref, wc_ref, wo_ref, fw_ref, final)


def _out_proj(x, norm_w, w_gate, layer, oa, yb, uc, wa, wb, wc, wo, final_w, final):
    m = x.shape[0]
    const = lambda shape: pl.BlockSpec(shape, lambda i: (0,) * len(shape))
    lw = lambda shape: pl.BlockSpec((None,) + shape, lambda i: (layer,) + (0,) * len(shape))
    return pl.pallas_call(
        functools.partial(_out_kernel, final=final),
        grid=(1,),
        in_specs=[const((m, D_MODEL)), const((1, D_MODEL)), lw((D_MODEL, GATE_PROJ_W)),
                  const((m, GDN_QK_W)), const((m, SSM_W)), const((m, CFM_W)),
                  lw((GDN_QK_W, D_MODEL)), lw((SSM_W, D_MODEL)), lw((CFM_W, D_MODEL)),
                  lw((D_MODEL, D_MODEL)), const((1, D_MODEL))],
        out_specs=const((m, D_MODEL)),
        out_shape=_sds((m, D_MODEL), F32),
        compiler_params=_params(1),
        name="merge_out",
    )(x, norm_w, w_gate, oa, yb, uc, wa, wb, wc, wo, final_w)


SAMPLE_TN = 384


def _proj_kernel(x_ref, nwin_ref, w_ref, o_ref):
    o_ref[...] = _norm_proj(x_ref, nwin_ref, w_ref)


def _sample_proj(x, norm_w, w, layer):
    m = x.shape[0]
    width = w.shape[-1]
    return pl.pallas_call(
        _proj_kernel,
        grid=(width // SAMPLE_TN,),
        in_specs=[pl.BlockSpec((m, D_MODEL), lambda j: (0, 0)),
                  pl.BlockSpec((1, D_MODEL), lambda j: (0, 0)),
                  pl.BlockSpec((None, D_MODEL, SAMPLE_TN), lambda j: (layer, 0, j))],
        out_specs=pl.BlockSpec((m, SAMPLE_TN), lambda j: (0, j)),
        out_shape=_sds((m, width), F32),
        compiler_params=_params(1),
        name="sample_proj",
    )(x, norm_w, w)


TOK_GROUP = 16
TOK_GROUP_SHIFT = 4


def _gdn_sample_kernel(p_ref, buf_ref, cw_ref, pc_ref, nw_ref, s_ref,
                       o_ref, so_ref, bo_ref,
                       kt_scr, qt_scr, v_scr, a_scr, b_scr, qk_scr, o_scr):
    tg = pl.program_id(0)

    @pl.when(tg == 0)
    def _():
        x = p_ref[:, GDN_COL_QKV:GDN_COL_QKV + GDN_CH]
        y = (cw_ref[0:1, :] * buf_ref[0] + cw_ref[1:2, :] * buf_ref[1]
             + cw_ref[2:3, :] * buf_ref[2] + cw_ref[3:4, :] * x)
        bo_ref[0] = buf_ref[1]
        bo_ref[1] = buf_ref[2]
        bo_ref[2] = x
        qkv = _silu(y)
        beta = _sigmoid(p_ref[:, GDN_COL_BA:GDN_COL_BA + LANES])
        decay = jnp.exp(-jnp.exp(pc_ref[0:1, :])
                        * _softplus(p_ref[:, GDN_COL_AA:GDN_COL_AA + LANES] + pc_ref[1:2, :]))
        n_tok = x.shape[0]
        for h in range(GDN_HEADS):
            q = qkv[:, h * GDN_D:(h + 1) * GDN_D]
            k = qkv[:, GDN_QK_W + h * GDN_D:GDN_QK_W + (h + 1) * GDN_D]
            q = q * lax.rsqrt(jnp.sum(q * q, axis=-1, keepdims=True) + EPS) * (GDN_D ** -0.5)
            k = k * lax.rsqrt(jnp.sum(k * k, axis=-1, keepdims=True) + EPS)
            kt_scr[h] = k.T
            qt_scr[h] = q.T
            a_scr[h] = jnp.broadcast_to(decay[:, h:h + 1], (n_tok, LANES))
            b_scr[h] = jnp.broadcast_to(beta[:, h:h + 1], (n_tok, LANES))
            qk_scr[h] = jnp.broadcast_to(jnp.sum(q * k, axis=-1, keepdims=True), (n_tok, LANES))
        v_scr[...] = qkv[:, 2 * GDN_QK_W:]

    shift = (LANES - TOK_GROUP * tg) & (LANES - 1)
    rows = pl.ds(pl.multiple_of(tg * TOK_GROUP, TOK_GROUP), TOK_GROUP)
    for h in range(GDN_HEADS):
        ktg = pltpu.roll(kt_scr[h], shift, axis=1)
        qtg = pltpu.roll(qt_scr[h], shift, axis=1)
        a_blk = a_scr[h, rows, :]
        b_blk = b_scr[h, rows, :]
        qk_blk = qk_scr[h, rows, :]
        v_blk = v_scr[rows, h * GDN_D:(h + 1) * GDN_D]
        o_rows = []
        for j in range(TOK_GROUP):
            s = s_ref[j, h]
            kcol = ktg[:, j:j + 1]
            qcol = qtg[:, j:j + 1]
            ks = jnp.sum(s * kcol, axis=0, keepdims=True)
            qs = jnp.sum(s * qcol, axis=0, keepdims=True)
            a_row = a_blk[j:j + 1, :]
            delta = b_blk[j:j + 1, :] * (v_blk[j:j + 1, :] - a_row * ks)
            so_ref[j, h] = a_row * s + kcol * delta
            o_rows.append(a_row * qs + qk_blk[j:j + 1, :] * delta)
        o_scr[rows, h * GDN_D:(h + 1) * GDN_D] = jnp.concatenate(o_rows, axis=0)

    @pl.when(tg == pl.num_programs(0) - 1)
    def _():
        o = o_scr[...]
        za = p_ref[:, GDN_COL_ZA:GDN_COL_ZA + GDN_QK_W]
        outs = []
        for h in range(GDN_HEADS):
            sl = slice(h * GDN_D, (h + 1) * GDN_D)
            outs.append(_rms(o[:, sl], nw_ref[...]) * _silu(za[:, sl]))
        o_ref[...] = jnp.concatenate(outs, axis=1).astype(BF16)


def _in_place_state(kernel_fn, inputs, in_specs, prev_out, out_index):
    if prev_out is None:
        return kernel_fn, inputs, in_specs, {}
    n = len(inputs)
    wrapped = lambda *refs: kernel_fn(*refs[:n], *refs[n + 1:])
    return wrapped, inputs + [prev_out], in_specs + [pl.BlockSpec(memory_space=pl.ANY)], {n: out_index}


def _gdn_sample(proj, buf_t, conv_w, pc, norm_w, state_all, layer, prev_out):
    n_tok = proj.shape[0]
    const = lambda shape: pl.BlockSpec(shape, lambda g: (0,) * len(shape))
    st_spec = pl.BlockSpec((None, TOK_GROUP, GDN_HEADS, GDN_D, GDN_D), lambda g: (layer, g, 0, 0, 0))
    in_specs = [const((n_tok, GDN_PROJ_W)), const((3, n_tok, GDN_CH)), const((4, GDN_CH)),
                const((8, LANES)), const((1, GDN_D)), st_spec]
    kern, inputs, in_specs, aliases = _in_place_state(
        _gdn_sample_kernel, [proj, buf_t, conv_w, pc, norm_w, state_all], in_specs, prev_out, 1)
    return pl.pallas_call(
        kern,
        grid=(n_tok // TOK_GROUP,),
        in_specs=in_specs,
        out_specs=[const((n_tok, GDN_QK_W)), st_spec, const((3, n_tok, GDN_CH))],
        out_shape=[_sds((n_tok, GDN_QK_W), BF16), _sds(state_all.shape, F32), _sds((3, n_tok, GDN_CH), F32)],
        input_output_aliases=aliases,
        scratch_shapes=[pltpu.VMEM((GDN_HEADS, GDN_D, n_tok), F32),
                        pltpu.VMEM((GDN_HEADS, GDN_D, n_tok), F32),
                        pltpu.VMEM((n_tok, GDN_QK_W), F32),
                        pltpu.VMEM((GDN_HEADS, n_tok, LANES), F32),
                        pltpu.VMEM((GDN_HEADS, n_tok, LANES), F32),
                        pltpu.VMEM((GDN_HEADS, n_tok, LANES), F32),
                        pltpu.VMEM((n_tok, GDN_QK_W), F32)],
        compiler_params=_params(1),
        name="gdn_sample",
    )(*inputs)


def _ssd_sample_kernel(p_ref, buf_ref, cw_ref, cb_ref, pc_ref, d_ref, nw_ref, s_ref,
                       o_ref, so_ref, bo_ref,
                       xt_scr, xs_scr, bm_scr, cm_scr, a_scr, yt_scr):
    tg = pl.program_id(0)
    n_tok = p_ref.shape[0]

    @pl.when(tg == 0)
    def _():
        x = p_ref[:, SSD_COL_XBC:SSD_COL_XBC + SSM_XBC]
        y = (cw_ref[0:1, :] * buf_ref[0] + cw_ref[1:2, :] * buf_ref[1]
             + cw_ref[2:3, :] * buf_ref[2] + cw_ref[3:4, :] * x)
        bo_ref[0] = buf_ref[1]
        bo_ref[1] = buf_ref[2]
        bo_ref[2] = x
        xbc = _silu(y + cb_ref[...])
        xs = xbc[:, :SSM_W]
        xs_scr[...] = xs
        bm_scr[...] = xbc[:, SSM_W:SSM_W + SSM_BC]
        cm_scr[...] = xbc[:, SSM_W + SSM_BC:]
        head_lane = _iota((1, LANES), 1) < SSM_HEADS
        dt = jnp.where(head_lane, _softplus(p_ref[:, SSD_COL_DT:SSD_COL_DT + LANES] + pc_ref[1:2, :]), 0.0)
        decay = jnp.exp(dt * (-jnp.exp(pc_ref[0:1, :])))
        xdt_t = (xs * _expand_heads(dt, _expand_matrix())).T
        hi = xdt_t.astype(BF16)
        xt_scr[:, :n_tok] = hi
        xt_scr[:, n_tok:] = (xdt_t - hi.astype(F32)).astype(BF16)
        for h in range(SSM_HEADS):
            a_scr[h] = jnp.broadcast_to(decay[:, h:h + 1], (n_tok, LANES))
        yt_scr[...] = jnp.zeros_like(yt_scr)

    rows = pl.ds(pl.multiple_of(tg * TOK_GROUP, TOK_GROUP), TOK_GROUP)
    bm_blk = bm_scr[rows, :]
    cm_blk = cm_scr[rows, :]
    lane = _iota((1, LANES), 1)
    piece_tok = _iota((2 * n_tok, LANES), 0) & (n_tok - 1)
    y_tile = jnp.zeros((SSM_W, LANES), F32)
    for j in range(TOK_GROUP):
        pick = jnp.where(piece_tok == tg * TOK_GROUP + j, 1.0, 0.0).astype(BF16)
        xb = jnp.dot(xt_scr[...], pick, preferred_element_type=F32)
        prods = []
        for h in range(SSM_HEADS):
            g = h // SSM_RPG
            s = s_ref[j, h]
            a_row = a_scr[h, rows, :][j:j + 1, :]
            xcol = xb[h * SSM_P:(h + 1) * SSM_P, :]
            s_new = a_row * s + xcol * bm_blk[j:j + 1, g * SSM_N:(g + 1) * SSM_N]
            so_ref[j, h] = s_new
            prods.append(s_new * cm_blk[j:j + 1, g * SSM_N:(g + 1) * SSM_N])
        y_col = jnp.sum(jnp.concatenate(prods, axis=0), axis=1, keepdims=True)
        y_tile = jnp.where(lane == j, y_col, y_tile)
    in_group = (lane >> TOK_GROUP_SHIFT) == tg
    yt_scr[...] = jnp.where(in_group, pltpu.roll(y_tile, TOK_GROUP * tg, axis=1), yt_scr[...])

    @pl.when(tg == pl.num_programs(0) - 1)
    def _():
        y = yt_scr[...].T + d_ref[...] * xs_scr[...]
        zb = p_ref[:, SSD_COL_ZB:SSD_COL_ZB + SSM_W]
        o_ref[...] = _rms(y * _silu(zb), nw_ref[...]).astype(BF16)


def _ssd_sample(proj, buf_t, conv_w, conv_b, pc, d_row, norm_w, state_all, layer, prev_out):
    n_tok = proj.shape[0]
    const = lambda shape: pl.BlockSpec(shape, lambda g: (0,) * len(shape))
    st_spec = pl.BlockSpec((None, TOK_GROUP, SSM_HEADS, SSM_P, SSM_N), lambda g: (layer, g, 0, 0, 0))
    in_specs = [const((n_tok, SSD_PROJ_W)), const((3, n_tok, SSM_XBC)), const((4, SSM_XBC)), const((1, SSM_XBC)),
                const((8, LANES)), const((1, SSM_W)), const((1, SSM_W)), st_spec]
    kern, inputs, in_specs, aliases = _in_place_state(
        _ssd_sample_kernel, [proj, buf_t, conv_w, conv_b, pc, d_row, norm_w, state_all], in_specs, prev_out, 1)
    return pl.pallas_call(
        kern,
        grid=(n_tok // TOK_GROUP,),
        in_specs=in_specs,
        out_specs=[const((n_tok, SSM_W)), st_spec, const((3, n_tok, SSM_XBC))],
        out_shape=[_sds((n_tok, SSM_W), BF16), _sds(state_all.shape, F32), _sds((3, n_tok, SSM_XBC), F32)],
        input_output_aliases=aliases,
        scratch_shapes=[pltpu.VMEM((SSM_W, 2 * n_tok), BF16),
                        pltpu.VMEM((n_tok, SSM_W), F32),
                        pltpu.VMEM((n_tok, SSM_BC), F32),
                        pltpu.VMEM((n_tok, SSM_BC), F32),
                        pltpu.VMEM((SSM_HEADS, n_tok, LANES), F32),
                        pltpu.VMEM((SSM_W, n_tok), F32)],
        compiler_params=_params(1),
        name="ssd_sample",
    )(*inputs)


def _cfm_sample_kernel(p_ref, buf_ref, cw_ref, cb_ref, lnw_ref, lnb_ref, o_ref, bo_ref):
    glu = p_ref[:, CFM_COL_GLU:CFM_COL_GLU + 2 * CFM_W]
    u0 = glu[:, :CFM_W] * _sigmoid(glu[:, CFM_W:])
    acc = cw_ref[CFM_K - 1:CFM_K, :] * u0 + cb_ref[...]
    for j in range(CFM_K - 1):
        row = buf_ref[j]
        acc = acc + cw_ref[j:j + 1, :] * row
        if j > 0:
            bo_ref[j - 1] = row
    bo_ref[CFM_K - 2] = u0
    gate = p_ref[:, CFM_COL_GC:CFM_COL_GC + CFM_W]
    o_ref[...] = _layernorm_silu_gate(acc, gate, lnw_ref[...], lnb_ref[...]).astype(BF16)


def _cfm_sample(proj, buf_t, conv_w, conv_b, ln_w, ln_b):
    n_tok = proj.shape[0]
    const = lambda shape: pl.BlockSpec(shape, lambda i: (0,) * len(shape))
    return pl.pallas_call(
        _cfm_sample_kernel,
        grid=(1,),
        in_specs=[const((n_tok, CFM_PROJ_W)), const((CFM_K - 1, n_tok, CFM_W)), const((CFM_K, CFM_W)),
                  const((1, CFM_W)), const((1, CFM_W)), const((1, CFM_W))],
        out_specs=[const((n_tok, CFM_W)), const((CFM_K - 1, n_tok, CFM_W))],
        out_shape=[_sds((n_tok, CFM_W), BF16), _sds((CFM_K - 1, n_tok, CFM_W), F32)],
        compiler_params=_params(1),
        name="cfm_sample",
    )(proj, buf_t, conv_w, conv_b, ln_w, ln_b)


_IN_Q, _IN_ZA, _IN_BA, _IN_AA, _IN_ZB, _IN_XBC, _IN_DT, _IN_GLU, _IN_GC, _IN_GATES, _IN_END = (
    0, 1536, 2048, 2052, 2056, 3080, 4616, 4632, 5656, 6168, 9240)


def _prep_w_in(w_in):
    seg = lambda a, b: w_in[:, :, a:b]
    zeros = lambda n: jnp.zeros(w_in.shape[:2] + (n,), w_in.dtype)
    w_gdn = jnp.concatenate([seg(_IN_Q, _IN_ZA), seg(_IN_ZA, _IN_BA),
                             seg(_IN_BA, _IN_AA), zeros(LANES - GDN_HEADS),
                             seg(_IN_AA, _IN_ZB), zeros(LANES - GDN_HEADS)], axis=-1)
    w_ssd = jnp.concatenate([seg(_IN_XBC, _IN_DT), seg(_IN_ZB, _IN_XBC),
                             seg(_IN_DT, _IN_GLU), zeros(LANES - SSM_HEADS)], axis=-1)
    w_cfm = seg(_IN_GLU, _IN_GATES)
    w_gate = seg(_IN_GATES, _IN_END)
    return tuple(w.astype(BF16) for w in (w_gdn, w_ssd, w_cfm, w_gate))


def _lane_pad(v, rows=8):
    out = jnp.zeros((v[0].shape[0], rows, LANES), F32)
    for i, a in enumerate(v):
        out = out.at[:, i, :a.shape[1]].set(a)
    return out


def kernel(x_prompt, x_sample, state_gdn, state_gdn_conv, state_ssm, state_ssm_conv, state_cfm_conv,
           norm_w, w_in, gdn_conv_w, gdn_a_log, gdn_dt_bias, gdn_norm_w, gdn_w_o,
           ssm_conv_w, ssm_conv_b, ssm_a_log, ssm_dt_bias, ssm_d, ssm_norm_w, ssm_w_o,
           cfm_conv_w, cfm_conv_b, cfm_ln_w, cfm_ln_b, cfm_w_o, w_out, final_norm_w):
    depth = w_in.shape[0]
    batch, seq, _ = x_prompt.shape
    n_tok = x_sample.shape[0]
    assert n_tok == LANES and seq % PROMPT_T == 0

    w_gdn, w_ssd, w_cfm, w_gate = _prep_w_in(w_in)
    wa, wb, wc, wo = (w.astype(BF16) for w in (gdn_w_o, ssm_w_o, cfm_w_o, w_out))
    gdn_pc = _lane_pad([gdn_a_log, gdn_dt_bias])
    ssm_pc = _lane_pad([ssm_a_log, ssm_dt_bias])
    ssm_d_row = jnp.repeat(ssm_d, SSM_P, axis=1)[:, None, :]
    final_w = final_norm_w[None, :]

    hp = x_prompt.reshape(batch * seq, D_MODEL)
    hs = x_sample.reshape(n_tok, D_MODEL)
    p_states, s_states = [], []
    new_gdn_s = new_ssm_s = None
    for i in range(depth):
        final = i == depth - 1
        row = lambda a: a[i][None, :]
        nw = row(norm_w)

        gdn_p = (gdn_conv_w[i], gdn_pc[i], row(gdn_norm_w))
        ssd_p = (ssm_conv_w[i], row(ssm_conv_b), ssm_pc[i], ssm_d_row[i], row(ssm_norm_w))
        cfm_p = (cfm_conv_w[i], row(cfm_conv_b), row(cfm_ln_w), row(cfm_ln_b))
        hp, gdn_s, gdn_cv, ssm_s, ssm_cv, cfm_cv = _prompt_layer(
            hp, nw, (w_gdn, w_ssd, w_cfm, w_gate, wa, wb, wc, wo), i, final_w, gdn_p, ssd_p, cfm_p,
            batch, seq, final)
        p_states.append((gdn_s, gdn_cv[:, CONV_HIST - 3:], ssm_s, ssm_cv[:, CONV_HIST - 3:],
                         cfm_cv[:, CFM_HIST - (CFM_K - 1):]))

        oa, new_gdn_s, gdn_buf = _gdn_sample(_sample_proj(hs, nw, w_gdn, i), jnp.swapaxes(state_gdn_conv[i], 0, 1),
                                             gdn_conv_w[i], gdn_pc[i], row(gdn_norm_w), state_gdn, i, new_gdn_s)
        yb, new_ssm_s, ssm_buf = _ssd_sample(_sample_proj(hs, nw, w_ssd, i), jnp.swapaxes(state_ssm_conv[i], 0, 1),
                                             ssm_conv_w[i], row(ssm_conv_b), ssm_pc[i], ssm_d_row[i],
                                             row(ssm_norm_w), state_ssm, i, new_ssm_s)
        uc, cfm_buf = _cfm_sample(_sample_proj(hs, nw, w_cfm, i), jnp.swapaxes(state_cfm_conv[i], 0, 1),
                                  cfm_conv_w[i], row(cfm_conv_b), row(cfm_ln_w), row(cfm_ln_b))
        hs = _out_proj(hs, nw, w_gate, i, oa, yb, uc, wa, wb, wc, wo, final_w, final)
        s_states.append((jnp.swapaxes(gdn_buf, 0, 1), jnp.swapaxes(ssm_buf, 0, 1), jnp.swapaxes(cfm_buf, 0, 1)))

    stack = lambda states, j: jnp.stack([s[j] for s in states])
    return (hp.reshape(batch, seq, D_MODEL), hs.reshape(n_tok, 1, D_MODEL),
            stack(p_states, 0), stack(p_states, 1), stack(p_states, 2), stack(p_states, 3), stack(p_states, 4),
            new_gdn_s, stack(s_states, 0), new_ssm_s, stack(s_states, 1), stack(s_states, 2))
```
